```python
import jax, jax.numpy as jnp
from jax import lax
import numpy as np

D_MODEL = 2048
BATCH = 16
SEQ = 256
DEPTH = 2
DEC_BATCH = 4
DEC_SEQ = 2048
PAST_LEN = 512

GRID_W = 64
N_EVEN = (DEPTH + 1) // 2
N_ODD = DEPTH // 2
EPS = 1e-6
ROPE_THETA = 10000.0
Q_BLOCK = 128

H_A = 8
NOPE_A = 128
ROPE_A = 64
V_A = 128
Q_LORA = 512
KV_LORA = 256
H_B = 8
KV_B = 2
HD_B = 128
OFF_AB = [Q_LORA, Q_LORA + KV_LORA, Q_LORA + KV_LORA + ROPE_A,
          Q_LORA + KV_LORA + ROPE_A + H_B * HD_B,
          Q_LORA + KV_LORA + ROPE_A + H_B * HD_B + KV_B * HD_B]
IN_AB = OFF_AB[-1] + KV_B * HD_B
MIX_AB = H_A * V_A + H_B * HD_B
H_C = 4
DK_C = D_MODEL // (2 * H_C)
DV_C = D_MODEL // H_C
HK_C = H_C * DK_C
HV_C = H_C * DV_C
GATE_RANK = 16
GATE_TAU = 16.0
CHUNK = 32
D_FF = 5632
CONV_W = 3

kernel_name = 'hybrid_flow_prefix_mla_gqa_gla_step'


def rmsnorm(x, g):
    xf = x.astype(jnp.float32)
    y = xf * lax.rsqrt(jnp.mean(xf * xf, axis=-1, keepdims=True) + EPS)
    return (y * g.astype(jnp.float32)).astype(x.dtype)


def modulation(cond, w_mod, b_mod):
    m = jax.nn.silu(cond) @ w_mod + b_mod
    return jnp.split(m[..., None, :], 6, axis=-1)


def modulate(x, shift, scale):
    return x * (1.0 + scale) + shift


def axial_angles(rows, rot_dim):
    n_freq = rot_dim // 4
    inv = ROPE_THETA ** (-jnp.arange(n_freq, dtype=jnp.float32) / n_freq)
    row = jnp.repeat(jnp.arange(rows, dtype=jnp.float32), GRID_W)
    col = jnp.tile(jnp.arange(GRID_W, dtype=jnp.float32), rows)
    ang_r = row[:, None] * inv
    ang_c = col[:, None] * inv
    ang = jnp.concatenate([ang_r, ang_r, ang_c, ang_c], axis=-1)
    return jnp.cos(ang), jnp.sin(ang)


def apply_axial_rope(x, rope):
    cos, sin = rope
    x1, x2, x3, x4 = jnp.split(x.astype(jnp.float32), 4, axis=-1)
    rot = jnp.concatenate([-x2, x1, -x4, x3], axis=-1)
    y = x.astype(jnp.float32) * cos[:, None, :] + rot * sin[:, None, :]
    return y.astype(x.dtype)


def blocked_attention(q, k, v, scale):
    B, S, H, Dq = q.shape
    G = k.shape[2]
    R = H // G
    nb = S // Q_BLOCK
    qb = q.reshape(B, nb, Q_BLOCK, G, R, Dq).transpose(1, 0, 2, 3, 4, 5)

    def one_block(qi):
        s = jnp.einsum('bqgrd,btgd->bgrqt', qi, k, preferred_element_type=jnp.float32) * scale
        p = jax.nn.softmax(s, axis=-1)
        return jnp.einsum('bgrqt,btgd->bqgrd', p.astype(v.dtype), v)

    o = lax.map(one_block, qb)
    return o.transpose(1, 0, 2, 3, 4, 5).reshape(B, S, H, v.shape[-1])


def ab_project(h, w_in, g_q, g_kv, w_uq, g_qn, g_kn, rope_a, rope_b):
    B, S, _ = h.shape
    cq, ckv, krope, qb, kb, vb = jnp.split(h @ w_in, OFF_AB, axis=-1)
    q_a = (rmsnorm(cq, g_q) @ w_uq).reshape(B, S, H_A, NOPE_A + ROPE_A)
    ckv = rmsnorm(ckv, g_kv)
    krope = krope.reshape(B, S, 1, ROPE_A)
    qb = rmsnorm(qb.reshape(B, S, H_B, HD_B), g_qn)
    kb = rmsnorm(kb.reshape(B, S, KV_B, HD_B), g_kn)
    vb = vb.reshape(B, S, KV_B, HD_B)
    if rope_a is not None:
        q_a = jnp.concatenate([q_a[..., :NOPE_A], apply_axial_rope(q_a[..., NOPE_A:], rope_a)], axis=-1)
        krope = apply_axial_rope(krope, rope_a)
        qb = apply_axial_rope(qb, rope_b)
        kb = apply_axial_rope(kb, rope_b)
    return q_a, ckv, krope[:, :, 0], qb, kb, vb


def ab_attend(q_a, ckv, krope, qb, kb, vb, w_ukv, w_out):
    B, T, _ = ckv.shape
    S = q_a.shape[1]
    kv = (ckv @ w_ukv).reshape(B, T, H_A, NOPE_A + V_A)
    k_a = jnp.concatenate([kv[..., :NOPE_A],
                           jnp.broadcast_to(krope[:, :, None, :], (B, T, H_A, ROPE_A))], axis=-1)
    o_a = blocked_attention(q_a, k_a, kv[..., NOPE_A:], (NOPE_A + ROPE_A) ** -0.5)
    o_b = blocked_attention(qb, kb, vb, HD_B ** -0.5)
    o = jnp.concatenate([o_a.reshape(B, S, H_A * V_A), o_b.reshape(B, S, H_B * HD_B)], axis=-1)
    return o @ w_out


def gla_project(h, w_in, wa_f, wb_f, ba_f, wa_b, wb_b, ba_b):
    B, S, _ = h.shape
    q, k, v, r = jnp.split(h @ w_in, [HK_C, 2 * HK_C, 2 * HK_C + HV_C], axis=-1)
    q = q.reshape(B, S, H_C, DK_C) * (DK_C ** -0.5)
    k = k.reshape(B, S, H_C, DK_C)
    v = v.reshape(B, S, H_C, DV_C)

    def log_gate(wa, wb, ba):
        z = ((h @ wa) @ wb + ba).astype(jnp.float32)
        return (jax.nn.log_sigmoid(z) / GATE_TAU).reshape(B, S, H_C, DK_C)

    return q, k, v, r, log_gate(wa_f, wb_f, ba_f), log_gate(wa_b, wb_b, ba_b)


def gla_chunk_scan(q, k, v, lg, s0):
    B, S, H, DK = q.shape
    DV = v.shape[-1]
    n = S // CHUNK

    def to_chunks(x):
        return x.astype(jnp.float32).reshape(B, n, CHUNK, H, x.shape[-1]).transpose(1, 0, 3, 2, 4)

    causal = jnp.tril(jnp.ones((CHUNK, CHUNK), dtype=bool))

    def step(s, inp):
        qi, ki, vi, gi = inp
        b = jnp.cumsum(gi, axis=2)
        diff = b[:, :, :, None, :] - b[:, :, None, :, :]
        decay = jnp.exp(jnp.where(causal[:, :, None], diff, -jnp.inf))
        a = jnp.einsum('bhtd,bhsd,bhtsd->bhts', qi, ki, decay)
        o = jnp.einsum('bhts,bhsv->bhtv', a, vi) + jnp.einsum('bhtd,bhdv->bhtv', qi * jnp.exp(b), s)
        b_last = b[:, :, -1:, :]
        s_new = jnp.exp(b_last[:, :, 0, :, None]) * s + jnp.einsum('bhsd,bhsv->bhdv', ki * jnp.exp(b_last - b), vi)
        return s_new, o

    s_fin, oc = lax.scan(step, s0.astype(jnp.float32), (to_chunks(q), to_chunks(k), to_chunks(v), to_chunks(lg)))
    o = oc.transpose(1, 0, 3, 2, 4).reshape(B, S, H, DV)
    return o, s_fin


def gla_bidir(q, k, v, lg_f, lg_b, s_f, s_b):
    flip = lambda x: x[:, ::-1]
    o_f, sf = gla_chunk_scan(q, k, v, lg_f, s_f)
    o_b, sb = gla_chunk_scan(flip(q), flip(k), flip(v), flip(lg_b), s_b)
    return o_f + flip(o_b), sf, sb


def gla_output(o, r, g_out, w_out):
    B, S = o.shape[:2]
    o = rmsnorm(o, g_out).astype(r.dtype).reshape(B, S, HV_C)
    return (o * jax.nn.silu(r)) @ w_out


def conv_ffn(h, w_up, conv_w, conv_b, w_down):
    u = h @ w_up
    up = jnp.pad(u, ((0, 0), (1, 1), (0, 0)))
    u = up[:, :-2] * conv_w[0] + up[:, 1:-1] * conv_w[1] + up[:, 2:] * conv_w[2] + conv_b
    a, g = jnp.split(u, 2, axis=-1)
    return (jax.nn.silu(g) * a) @ w_down


def ctx_cat(ctx, lat):
    return jnp.concatenate([ctx.astype(lat.dtype), lat], axis=1)


def setup_inputs(seed: int = 0) -> dict:
    key = jax.random.key(seed)
    ks = iter(jax.random.split(key, 48))

    def nrm(shape, scale=1.0):
        return jax.random.normal(next(ks), shape, jnp.float32) * scale

    def gain(shape):
        return 1.0 + nrm(shape, 0.01)

    D = D_MODEL
    return {
        'x_prompt': nrm((BATCH, SEQ, D)),
        'x_sample': nrm((DEC_BATCH, DEC_SEQ, D)),
        'cache_mla_ckv': nrm((DEC_BATCH, N_EVEN, PAST_LEN, KV_LORA)),
        'cache_mla_krope': nrm((DEC_BATCH, N_EVEN, PAST_LEN, ROPE_A)),
        'cache_gqa_k': nrm((DEC_BATCH, N_EVEN, PAST_LEN, KV_B, HD_B)),
        'cache_gqa_v': nrm((DEC_BATCH, N_EVEN, PAST_LEN, KV_B, HD_B)),
        'state_gla_fwd': nrm((DEC_BATCH, N_ODD, H_C, DK_C, DV_C)),
        'state_gla_bwd': nrm((DEC_BATCH, N_ODD, H_C, DK_C, DV_C)),
        'c': nrm((DEC_BATCH, D)),
        'c_ctx': nrm((D,)),
        'w_mod': nrm((DEPTH, D, 6 * D), 0.5 * D ** -0.5),
        'b_mod': nrm((DEPTH, 6 * D), 0.01),
        'g_pre_mix': gain((DEPTH, D)),
        'g_post_mix': gain((DEPTH, D)),
        'g_pre_ffn': gain((DEPTH, D)),
        'g_post_ffn': gain((DEPTH, D)),
        'w_in_ab': nrm((N_EVEN, D, IN_AB), D ** -0.5),
        'g_mla_q': gain((N_EVEN, Q_LORA)),
        'g_mla_kv': gain((N_EVEN, KV_LORA)),
        'w_mla_uq': nrm((N_EVEN, Q_LORA, H_A * (NOPE_A + ROPE_A)), Q_LORA ** -0.5),
        'w_mla_ukv': nrm((N_EVEN, KV_LORA, H_A * (NOPE_A + V_A)), KV_LORA ** -0.5),
        'g_gqa_q': gain((N_EVEN, HD_B)),
        'g_gqa_k': gain((N_EVEN, HD_B)),
        'w_out_ab': nrm((N_EVEN, MIX_AB, D), MIX_AB ** -0.5),
        'w_in_c': nrm((N_ODD, D, 2 * HK_C + 2 * HV_C), D ** -0.5),
        'w_gate_fwd_a': nrm((N_ODD, D, GATE_RANK), D ** -0.5),
        'w_gate_fwd_b': nrm((N_ODD, GATE_RANK, HK_C), GATE_RANK ** -0.5),
        'b_gate_fwd': nrm((N_ODD, HK_C), 0.01),
        'w_gate_bwd_a': nrm((N_ODD, D, GATE_RANK), D ** -0.5),
        'w_gate_bwd_b': nrm((N_ODD, GATE_RANK, HK_C), GATE_RANK ** -0.5),
        'b_gate_bwd': nrm((N_ODD, HK_C), 0.01),
        'g_gla_out': gain((N_ODD, DV_C)),
        'w_out_c': nrm((N_ODD, HV_C, D), HV_C ** -0.5),
        'w_ffn_up': nrm((DEPTH, D, 2 * D_FF), D ** -0.5),
        'ffn_conv_w': nrm((DEPTH, CONV_W, 2 * D_FF), CONV_W ** -0.5),
        'ffn_conv_b': nrm((DEPTH, 2 * D_FF), 0.01),
        'w_ffn_down': nrm((DEPTH, D_FF, D), D_FF ** -0.5),
    }


def reference(x_prompt, x_sample, cache_mla_ckv, cache_mla_krope, cache_gqa_k, cache_gqa_v,
              state_gla_fwd, state_gla_bwd, c, c_ctx, w_mod, b_mod, g_pre_mix, g_post_mix,
              g_pre_ffn, g_post_ffn, w_in_ab, g_mla_q, g_mla_kv, w_mla_uq, w_mla_ukv, g_gqa_q,
              g_gqa_k, w_out_ab, w_in_c, w_gate_fwd_a, w_gate_fwd_b, b_gate_fwd, w_gate_bwd_a,
              w_gate_bwd_b, b_gate_bwd, g_gla_out, w_out_c, w_ffn_up, ffn_conv_w, ffn_conv_b,
              w_ffn_down):
    n_lat = x_sample.shape[1]
    rows = n_lat // GRID_W
    rope_a = axial_angles(rows, ROPE_A)
    rope_b = axial_angles(rows, HD_B)
    xp, xs = x_prompt, x_sample
    ckv_l, krope_l, k_l, v_l, sf_l, sb_l = [], [], [], [], [], []
    for i in range(DEPTH):
        sh1p, sc1p, gt1p, sh2p, sc2p, gt2p = modulation(c_ctx, w_mod[i], b_mod[i])
        sh1s, sc1s, gt1s, sh2s, sc2s, gt2s = modulation(c, w_mod[i], b_mod[i])
        hp = modulate(rmsnorm(xp, g_pre_mix[i]), sh1p, sc1p)
        hs = modulate(rmsnorm(xs, g_pre_mix[i]), sh1s, sc1s)
        j = i // 2
        if i % 2 == 0:
            qa_p, ckv_p, kr_p, qb_p, kb_p, vb_p = ab_project(
                hp, w_in_ab[j], g_mla_q[j], g_mla_kv[j], w_mla_uq[j], g_gqa_q[j], g_gqa_k[j], None, None)
            yp = ab_attend(qa_p, ckv_p, kr_p, qb_p, kb_p, vb_p, w_mla_ukv[j], w_out_ab[j])
            qa_s, ckv_s, kr_s, qb_s, kb_s, vb_s = ab_project(
                hs, w_in_ab[j], g_mla_q[j], g_mla_kv[j], w_mla_uq[j], g_gqa_q[j], g_gqa_k[j], rope_a, rope_b)
            ys = ab_attend(qa_s, ctx_cat(cache_mla_ckv[:, j], ckv_s), ctx_cat(cache_mla_krope[:, j], kr_s),
                           qb_s, ctx_cat(cache_gqa_k[:, j], kb_s), ctx_cat(cache_gqa_v[:, j], vb_s),
                           w_mla_ukv[j], w_out_ab[j])
            ckv_l.append(ckv_p)
            krope_l.append(kr_p)
            k_l.append(kb_p)
            v_l.append(vb_p)
        else:
            q_p, k_p, v_p, r_p, lf_p, lb_p = gla_project(
                hp, w_in_c[j], w_gate_fwd_a[j], w_gate_fwd_b[j], b_gate_fwd[j],
                w_gate_bwd_a[j], w_gate_bwd_b[j], b_gate_bwd[j])
            zero = jnp.zeros((xp.shape[0], H_C, DK_C, DV_C), jnp.float32)
            o_p, sf_p, sb_p = gla_bidir(q_p, k_p, v_p, lf_p, lb_p, zero, zero)
            yp = gla_output(o_p, r_p, g_gla_out[j], w_out_c[j])
            q_s, k_s, v_s, r_s, lf_s, lb_s = gla_project(
                hs, w_in_c[j], w_gate_fwd_a[j], w_gate_fwd_b[j], b_gate_fwd[j],
                w_gate_bwd_a[j], w_gate_bwd_b[j], b_gate_bwd[j])
            o_s, _, _ = gla_bidir(q_s, k_s, v_s, lf_s, lb_s, state_gla_fwd[:, j], state_gla_bwd[:, j])
            ys = gla_output(o_s, r_s, g_gla_out[j], w_out_c[j])
            sf_l.append(sf_p)
            sb_l.append(sb_p)
        xp = xp + gt1p * rmsnorm(yp, g_post_mix[i])
        xs = xs + gt1s * rmsnorm(ys, g_post_mix[i])
        hp = modulate(rmsnorm(xp, g_pre_ffn[i]), sh2p, sc2p)
        hs = modulate(rmsnorm(xs, g_pre_ffn[i]), sh2s, sc2s)
        xp = xp + gt2p * rmsnorm(conv_ffn(hp, w_ffn_up[i], ffn_conv_w[i], ffn_conv_b[i], w_ffn_down[i]), g_post_ffn[i])
        xs = xs + gt2s * rmsnorm(conv_ffn(hs, w_ffn_up[i], ffn_conv_w[i], ffn_conv_b[i], w_ffn_down[i]), g_post_ffn[i])
    new_mla_ckv = jnp.stack(ckv_l, axis=1)
    new_mla_krope = jnp.stack(krope_l, axis=1)
    new_gqa_k = jnp.stack(k_l, axis=1)
    new_gqa_v = jnp.stack(v_l, axis=1)
    new_gla_fwd = jnp.stack(sf_l, axis=1)
    new_gla_bwd = jnp.stack(sb_l, axis=1)
    return (xp, xs, new_mla_ckv, new_mla_krope, new_gqa_k, new_gqa_v, new_gla_fwd, new_gla_bwd)
```

```python
import functools
import math

import numpy as np
import jax
import jax.numpy as jnp
from jax import lax
from jax.experimental import pallas as pl
from jax.experimental.pallas import tpu as pltpu

D = 2048
N_PROMPT_SEQ, PROMPT_LEN = 16, 256
N_SAMPLE_SEQ, SAMPLE_LEN = 4, 2048
PAST = 512
GRID_W = 64
NP = N_PROMPT_SEQ * PROMPT_LEN
NS = N_SAMPLE_SEQ * SAMPLE_LEN
M_ROWS = NP + NS
EPS = 1e-6
ROPE_THETA = 10000.0

H_A, NOPE_A, ROPE_A, V_A = 8, 128, 64, 128
Q_LORA, KV_LORA = 512, 256
H_B, KV_B, HD_B = 8, 2, 128
IN_AB_PAD = 2560
H_C, DK_C, DV_C = 4, 256, 512
HK_C, HV_C = H_C * DK_C, H_C * DV_C
GATE_RANK = 16
GATE_TAU = 16.0
IN_C_PAD = 2 * HK_C + 2 * HV_C + 128
D_FF = 5632
GLA_CHUNK = 128
N_MOD_ROWS = 8

VMEM_LIMIT_BYTES = 56 * 1024 * 1024
BF16 = jnp.bfloat16
F32 = jnp.float32


def _params(*sem):
    return pltpu.CompilerParams(dimension_semantics=sem, vmem_limit_bytes=VMEM_LIMIT_BYTES)


def _mod_row(i, tm):
    n_prompt_tiles = NP // tm
    per_seq = SAMPLE_LEN // tm
    return jnp.where(i < n_prompt_tiles, 0, 1 + (i - n_prompt_tiles) // per_seq)


def _mod_spec(layer, piece, tm, row_axis=0):
    def index_map(*ids):
        return ((layer * N_MOD_ROWS + _mod_row(ids[row_axis], tm)) * 6 + piece, 0, 0)
    return pl.BlockSpec((None, 1, D), index_map)


def _rms(x, g):
    return x * lax.rsqrt(jnp.mean(x * x, axis=-1, keepdims=True) + EPS) * g


def _dot(a, b):
    return jnp.dot(a, b, preferred_element_type=F32)


def _dot_nt(a, b):
    return lax.dot_general(a, b, (((1,), (1,)), ((), ())), preferred_element_type=F32)


def _dot_tn(a, b):
    return lax.dot_general(a, b, (((0,), (0,)), ((), ())), preferred_element_type=F32)


def _split2(x):
    hi = x.astype(BF16)
    lo = (x - hi.astype(F32)).astype(BF16)
    return hi, lo


def _mod_kernel(c_ref, w_ref, b_ref, o_ref):
    c = c_ref[...]
    a = c * (1.0 / (1.0 + jnp.exp(-c)))
    a_hi, a_lo = _split2(a)
    w_hi, w_lo = _split2(w_ref[...])
    r = _dot(jnp.concatenate([a_hi, a_lo], axis=0), w_hi)
    o_ref[...] = r[:N_MOD_ROWS] + r[N_MOD_ROWS:] + _dot(a_hi, w_lo) + b_ref[...]


def _modulation(cond, w_mod, b_mod):
    depth = w_mod.shape[0]
    n = w_mod.shape[2]
    tn = 1024
    return pl.pallas_call(
        _mod_kernel,
        grid=(depth, n // tn),
        in_specs=[pl.BlockSpec((N_MOD_ROWS, D), lambda l, j: (0, 0)),
                  pl.BlockSpec((None, D, tn), lambda l, j: (l, 0, j)),
                  pl.BlockSpec((None, 1, tn), lambda l, j: (l, 0, j))],
        out_specs=pl.BlockSpec((None, N_MOD_ROWS, tn), lambda l, j: (l, 0, j)),
        out_shape=jax.ShapeDtypeStruct((depth, N_MOD_ROWS, n), F32),
        compiler_params=_params("arbitrary", "arbitrary"),
        name="modulation",
    )(cond, w_mod, b_mod.reshape(depth, 1, n))


def _prenorm_kernel(x_ref, g_ref, sh_ref, sc_ref, o_ref):
    h = _rms(x_ref[...], g_ref[...]) * (1.0 + sc_ref[...]) + sh_ref[...]
    o_ref[...] = h.astype(o_ref.dtype)


def _prenorm(x, g, mod, layer):
    tm = 512
    return pl.pallas_call(
        _prenorm_kernel,
        grid=(M_ROWS // tm,),
        in_specs=[pl.BlockSpec((tm, D), lambda i: (i, 0)),
                  pl.BlockSpec((1, D), lambda i: (0, 0)),
                  _mod_spec(layer, 0, tm), _mod_spec(layer, 1, tm)],
        out_specs=pl.BlockSpec((tm, D), lambda i: (i, 0)),
        out_shape=jax.ShapeDtypeStruct((M_ROWS, D), BF16),
        compiler_params=_params("arbitrary"),
        name="prenorm",
    )(x, g.reshape(1, D), mod, mod)


def _matmul_kernel(x_ref, w_ref, o_ref):
    o_ref[...] = _dot(x_ref[...].astype(BF16), w_ref[...].astype(BF16)).astype(o_ref.dtype)


def _matmul(x, w, *, tm, tn, out_dtype, name):
    m, k = x.shape
    n = w.shape[1]
    return pl.pallas_call(
        _matmul_kernel,
        grid=(n // tn, m // tm),
        in_specs=[pl.BlockSpec((tm, k), lambda j, i: (i, 0)),
                  pl.BlockSpec((k, tn), lambda j, i: (0, j))],
        out_specs=pl.BlockSpec((tm, tn), lambda j, i: (i, j)),
        out_shape=jax.ShapeDtypeStruct((m, n), out_dtype),
        compiler_params=_params("arbitrary", "arbitrary"),
        name=name,
    )(x, w)


def _rope_tables(rot_dim, tm):
    q = rot_dim // 4
    inv = ROPE_THETA ** (-np.arange(q, dtype=np.float64) / q)
    tok = np.arange(SAMPLE_LEN)
    ang_r = (tok // GRID_W)[:, None] * inv
    ang_c = (tok % GRID_W)[:, None] * inv
    ang = np.concatenate([ang_r, ang_r, ang_c, ang_c], axis=-1)
    cos, sin = np.cos(ang), np.sin(ang)
    first = (np.arange(rot_dim) % (2 * q)) < q
    sin_m = np.where(first, -sin, 0.0)
    sin_p = np.where(first, 0.0, sin)

    def widen(t, ident):
        full = np.full((tm + SAMPLE_LEN, 128), ident, np.float64)
        full[tm:, :rot_dim] = t
        if rot_dim < 128:
            full[tm:, rot_dim:] = ident
        return jnp.asarray(full, F32)

    return widen(cos, 1.0), widen(sin_m, 0.0), widen(sin_p, 0.0), q


def _rope128(x, cos, sin_m, sin_p, q):
    return (x * cos + pltpu.roll(x, 128 - q, 1) * sin_m + pltpu.roll(x, q, 1) * sin_p)


def _ab_post_kernel(y_ref, gq_ref, gkv_ref, gqn_ref, gkn_ref, wuq_ref,
                    ca_ref, ma_ref, pa_ref, cb_ref, mb_ref, pb_ref,
                    qa_ref, qb_ref, ckv_ref, kr_ref, kb_ref, vb_ref, *, qa_rot, qb_rot):
    ca, ma, pa = ca_ref[...], ma_ref[...], pa_ref[...]
    cb, mb, pb = cb_ref[...], mb_ref[...], pb_ref[...]
    scale_a = (NOPE_A + ROPE_A) ** -0.5
    scale_b = HD_B ** -0.5

    cq = _rms(y_ref[:, 0:Q_LORA], gq_ref[...])
    qa = _dot(cq.astype(BF16), wuq_ref[...].astype(BF16)) * scale_a
    for h in range(H_A):
        base = h * 256
        qa_ref[:, base:base + 128] = qa[:, base:base + 128].astype(qa_ref.dtype)
        qa_ref[:, base + 128:base + 256] = _rope128(
            qa[:, base + 128:base + 256], ca, ma, pa, qa_rot).astype(qa_ref.dtype)

    ckv_ref[...] = _rms(y_ref[:, 512:768], gkv_ref[...])
    kr_ref[...] = _rope128(y_ref[:, 2304:2432], ca, ma, pa, qa_rot)

    for h in range(H_B):
        xb = _rms(y_ref[:, 768 + h * 128:768 + (h + 1) * 128], gqn_ref[...])
        qb_ref[:, h * 128:(h + 1) * 128] = (
            _rope128(xb, cb, mb, pb, qb_rot) * scale_b).astype(qb_ref.dtype)
    for g in range(KV_B):
        xk = _rms(y_ref[:, 1792 + g * 128:1792 + (g + 1) * 128], gkn_ref[...])
        kb_ref[:, g * 128:(g + 1) * 128] = _rope128(xk, cb, mb, pb, qb_rot)
    vb_ref[...] = y_ref[:, 2048:2304]


def _ab_post(y, g_q, g_kv, g_qn, g_kn, w_uq_pad):
    tm = 256
    ca, ma, pa, qa_rot = _rope_tables(ROPE_A, tm)
    cb, mb, pb, qb_rot = _rope_tables(HD_B, tm)
    n_prompt_tiles = NP // tm
    per_seq = SAMPLE_LEN // tm

    def tab_map(i):
        return (jnp.where(i < n_prompt_tiles, 0, 1 + (i - n_prompt_tiles) % per_seq), 0)

    tab = pl.BlockSpec((tm, 128), tab_map)
    row = lambda w: pl.BlockSpec((tm, w), lambda i: (i, 0))
    vec = lambda w: pl.BlockSpec((1, w), lambda i: (0, 0))
    return pl.pallas_call(
        functools.partial(_ab_post_kernel, qa_rot=qa_rot, qb_rot=qb_rot),
        grid=(M_ROWS // tm,),
        in_specs=[row(IN_AB_PAD), vec(Q_LORA), vec(KV_LORA), vec(HD_B), vec(HD_B),
                  pl.BlockSpec((Q_LORA, H_A * 256), lambda i: (0, 0)),
                  tab, tab, tab, tab, tab, tab],
        out_specs=[row(H_A * 256), row(H_B * HD_B), row(KV_LORA), row(128),
                   row(KV_B * HD_B), row(KV_B * HD_B)],
        out_shape=[jax.ShapeDtypeStruct((M_ROWS, H_A * 256), BF16),
                   jax.ShapeDtypeStruct((M_ROWS, H_B * HD_B), BF16),
                   jax.ShapeDtypeStruct((M_ROWS, KV_LORA), F32),
                   jax.ShapeDtypeStruct((M_ROWS, 128), F32),
                   jax.ShapeDtypeStruct((M_ROWS, KV_B * HD_B), F32),
                   jax.ShapeDtypeStruct((M_ROWS, KV_B * HD_B), F32)],
        compiler_params=_params("arbitrary"),
        name="ab_post",
    )(y, g_q.reshape(1, -1), g_kv.reshape(1, -1), g_qn.reshape(1, -1), g_kn.reshape(1, -1),
      w_uq_pad, ca, ma, pa, cb, mb, pb)


def _softmax_pv(s, v):
    m = jnp.max(s, axis=-1, keepdims=True)
    e = jnp.exp(s - m)
    l = jnp.sum(e, axis=-1, keepdims=True)
    return _dot(e.astype(BF16), v) / l


def _attn_kernel(qa_ref, qb_ref, kv_ref, kr_ref, kb_ref, vb_ref, o_ref):
    kr = kr_ref[...]
    for h in range(H_A):
        k = jnp.concatenate([kv_ref[:, h * 256:h * 256 + 128], kr], axis=1)
        s = _dot_nt(qa_ref[:, h * 256:(h + 1) * 256], k)
        o = _softmax_pv(s, kv_ref[:, h * 256 + 128:(h + 1) * 256])
        o_ref[:, h * 128:(h + 1) * 128] = o.astype(o_ref.dtype)
    rep = H_B // KV_B
    for h in range(H_B):
        g = h // rep
        s = _dot_nt(qb_ref[:, h * 128:(h + 1) * 128], kb_ref[:, g * 128:(g + 1) * 128])
        o = _softmax_pv(s, vb_ref[:, g * 128:(g + 1) * 128])
        o_ref[:, (H_A + h) * 128:(H_A + h + 1) * 128] = o.astype(o_ref.dtype)


def _attention(qa, qb, kv, kr, kb, vb, *, n_seq, s_len, t_len, q_row0, name):
    tq = 256
    nq = s_len // tq
    q0 = q_row0 // tq
    qspec = lambda w: pl.BlockSpec((tq, w), lambda b, i: (q0 + b * nq + i, 0))
    kspec = lambda w: pl.BlockSpec((t_len, w), lambda b, i: (b, 0), pipeline_mode=pl.Buffered(1))
    return pl.pallas_call(
        _attn_kernel,
        grid=(n_seq, nq),
        in_specs=[qspec(H_A * 256), qspec(H_B * HD_B), kspec(H_A * 256), kspec(128),
                  kspec(KV_B * HD_B), kspec(KV_B * HD_B)],
        out_specs=pl.BlockSpec((tq, D), lambda b, i: (b * nq + i, 0)),
        out_shape=jax.ShapeDtypeStruct((n_seq * s_len, D), BF16),
        compiler_params=_params("arbitrary", "arbitrary"),
        name=name,
    )(qa, qb, kv, kr, kb, vb)


def _residual_epilogue(acc, x_ref, gpost_ref, gate_ref, nxt, xo_ref, ho_ref):
    x_new = x_ref[...] + gate_ref[...] * _rms(acc, gpost_ref[...])
    xo_ref[...] = x_new
    if nxt is not None:
        gn_ref, sh_ref, sc_ref = nxt
        ho_ref[...] = (_rms(x_new, gn_ref[...]) * (1.0 + sc_ref[...]) + sh_ref[...]).astype(ho_ref.dtype)


def _proj_res_kernel(*refs, emit_h):
    if emit_h:
        a_ref, w_ref, x_ref, gpost_ref, gate_ref, gn_ref, sh_ref, sc_ref, xo_ref, ho_ref, acc_ref = refs
        nxt = (gn_ref, sh_ref, sc_ref)
    else:
        a_ref, w_ref, x_ref, gpost_ref, gate_ref, xo_ref, acc_ref = refs
        nxt, ho_ref = None, None
    k = pl.program_id(1)

    @pl.when(k == 0)
    def _():
        acc_ref[...] = jnp.zeros_like(acc_ref)

    acc_ref[...] += _dot(a_ref[...], w_ref[...].astype(BF16))

    @pl.when(k == pl.num_programs(1) - 1)
    def _():
        _residual_epilogue(acc_ref[...], x_ref, gpost_ref, gate_ref, nxt, xo_ref, ho_ref)


def _res_specs(tm, mod, layer, gate_piece, g_post, nxt):
    vec = pl.BlockSpec((1, D), lambda i, k: (0, 0))
    specs = [pl.BlockSpec((tm, D), lambda i, k: (i, 0)), vec, _mod_spec(layer, gate_piece, tm)]
    args = [g_post.reshape(1, D), mod]
    if nxt is not None:
        g_next, layer_next, sh_piece, sc_piece = nxt
        specs += [vec, _mod_spec(layer_next, sh_piece, tm), _mod_spec(layer_next, sc_piece, tm)]
        args += [g_next.reshape(1, D), mod, mod]
    out_specs = [pl.BlockSpec((tm, D), lambda i, k: (i, 0))]
    out_shape = [jax.ShapeDtypeStruct((M_ROWS, D), F32)]
    if nxt is not None:
        out_specs.append(pl.BlockSpec((tm, D), lambda i, k: (i, 0)))
        out_shape.append(jax.ShapeDtypeStruct((M_ROWS, D), BF16))
    return specs, args, out_specs, out_shape


def _proj_res(a, w, x, mod, layer, gate_piece, g_post, nxt, *, name):
    tm, tk = 512, 512
    kdim = a.shape[1]
    specs, args, out_specs, out_shape = _res_specs(tm, mod, layer, gate_piece, g_post, nxt)
    out = pl.pallas_call(
        functools.partial(_proj_res_kernel, emit_h=nxt is not None),
        grid=(M_ROWS // tm, kdim // tk),
        in_specs=[pl.BlockSpec((tm, tk), lambda i, k: (i, k)),
                  pl.BlockSpec((tk, D), lambda i, k: (k, 0))] + specs,
        out_specs=out_specs,
        out_shape=out_shape,
        scratch_shapes=[pltpu.VMEM((tm, D), F32)],
        compiler_params=_params("arbitrary", "arbitrary"),
        name=name,
    )(a, w, x, *args)
    return out if nxt is not None else (out[0], None)


def _ffn_up_kernel(h_ref, hp_ref, hn_ref, wa_ref, wg_ref, cwa_ref, cwg_ref, cba_ref, cbg_ref,
                   mp_ref, mn_ref, o_ref):
    tm = h_ref.shape[0]
    h = h_ref[...]
    hp = hp_ref[...]
    hn = hn_ref[...]
    mp = mp_ref[...]
    mn = mn_ref[...]
    row = lax.broadcasted_iota(jnp.int32, (tm, 1), 0)
    halo = hp_ref.shape[0]

    def branch(w_ref, cw_ref, cb_ref):
        w = w_ref[...].astype(BF16)
        u = _dot(h, w)
        u_first = _dot(hp, w)[halo - 1:halo]
        u_last = _dot(hn, w)[0:1]
        prev = jnp.where(row == 0, u_first, pltpu.roll(u, 1, 0)) * mp
        nxt = jnp.where(row == tm - 1, u_last, pltpu.roll(u, tm - 1, 0)) * mn
        cw = cw_ref[...]
        return prev * cw[0:1] + u * cw[1:2] + nxt * cw[2:3] + cb_ref[...]

    a = branch(wa_ref, cwa_ref, cba_ref)
    g = branch(wg_ref, cwg_ref, cbg_ref)
    o_ref[...] = (g * (1.0 / (1.0 + jnp.exp(-g))) * a).astype(o_ref.dtype)


def _seq_edge_masks():
    pos = np.concatenate([np.tile(np.arange(PROMPT_LEN), N_PROMPT_SEQ),
                          np.tile(np.arange(SAMPLE_LEN), N_SAMPLE_SEQ)])
    lens = np.concatenate([np.full(NP, PROMPT_LEN), np.full(NS, SAMPLE_LEN)])
    has_prev = (pos > 0).astype(np.float32).reshape(M_ROWS, 1)
    has_next = (pos < lens - 1).astype(np.float32).reshape(M_ROWS, 1)
    return jnp.asarray(has_prev), jnp.asarray(has_next)


def _ffn_up(h, w_up, conv_w, conv_b):
    tm, tn, halo = 1024, 512, 16
    nj = D_FF // tn
    r = tm // halo
    last = M_ROWS // halo - 1
    has_prev, has_next = _seq_edge_masks()
    conv_b = conv_b.reshape(1, 2 * D_FF)
    return pl.pallas_call(
        _ffn_up_kernel,
        grid=(nj, M_ROWS // tm),
        in_specs=[pl.BlockSpec((tm, D), lambda j, i: (i, 0)),
                  pl.BlockSpec((halo, D), lambda j, i: (jnp.maximum(i * r - 1, 0), 0)),
                  pl.BlockSpec((halo, D), lambda j, i: (jnp.minimum((i + 1) * r, last), 0)),
                  pl.BlockSpec((D, tn), lambda j, i: (0, j)),
                  pl.BlockSpec((D, tn), lambda j, i: (0, j + nj)),
                  pl.BlockSpec((3, tn), lambda j, i: (0, j)),
                  pl.BlockSpec((3, tn), lambda j, i: (0, j + nj)),
                  pl.BlockSpec((1, tn), lambda j, i: (0, j)),
                  pl.BlockSpec((1, tn), lambda j, i: (0, j + nj)),
                  pl.BlockSpec((tm, 1), lambda j, i: (i, 0)),
                  pl.BlockSpec((tm, 1), lambda j, i: (i, 0))],
        out_specs=pl.BlockSpec((tm, tn), lambda j, i: (i, j)),
        out_shape=jax.ShapeDtypeStruct((M_ROWS, D_FF), BF16),
        compiler_params=_params("arbitrary", "arbitrary"),
        name="ffn_up",
    )(h, h, h, w_up, w_up, conv_w, conv_w, conv_b, conv_b, has_prev, has_next)


def _gla_consts(c, reverse):
    levels = int(math.log2(c))
    t = np.arange(c)
    mats = [t[None, :] <= t[:, None]]
    masks = [t[:, None] == t[None, :]]
    for l in range(1, levels + 1):
        blk = t >> l
        split = (blk << l) + (1 << (l - 1)) - 1
        lo, hi = np.minimum(t, split), np.maximum(t, split)
        mats.append((t[None, :] > lo[:, None]) & (t[None, :] <= hi[:, None]))
        upper = ((t >> (l - 1)) & 1) == 1
        masks.append((blk[:, None] == blk[None, :]) & upper[:, None] & ~upper[None, :])
    if reverse:
        mats = [m[::-1, ::-1] for m in mats]
        masks = [m[::-1, ::-1] for m in masks]
    mstack = np.concatenate(mats, axis=0).astype(np.float32)
    return jnp.asarray(mstack, BF16), jnp.asarray(np.stack(masks).astype(np.float32))


def _log_sigmoid(z):
    return jnp.minimum(z, 0.0) - jnp.log1p(jnp.exp(-jnp.abs(z)))


def _gla_direction(q_ref, k_ref, v_ref, za, wb_ref, ba_ref, m_ref, mask_ref, st_ref, total_row):
    c = q_ref.shape[0]
    levels = m_ref.shape[0] // c - 1
    q = q_ref[...] * (DK_C ** -0.5)
    k = k_ref[...]
    v = v_ref[...].astype(BF16)

    za_hi, za_lo = _split2(za)
    wb_hi, wb_lo = _split2(wb_ref[...])
    z = _dot(za_hi, wb_hi) + _dot(za_lo, wb_hi) + _dot(za_hi, wb_lo) + ba_ref[...]
    g = _log_sigmoid(z) * (1.0 / GATE_TAU)

    g1 = g.astype(BF16)
    r1 = g - g1.astype(F32)
    g2 = r1.astype(BF16)
    g3 = (r1 - g2.astype(F32)).astype(BF16)
    e_all = _dot(m_ref[...], jnp.concatenate([g1, g2, g3], axis=1))
    expo = e_all[:, 0:DK_C] + e_all[:, DK_C:2 * DK_C] + e_all[:, 2 * DK_C:3 * DK_C]

    b = expo[0:c]
    b_tot = b[total_row:total_row + 1]

    a = mask_ref[0] * _dot_nt(q.astype(BF16), k.astype(BF16))
    for l in range(1, levels + 1):
        e = jnp.exp(expo[l * c:(l + 1) * c])
        a = a + mask_ref[l] * _dot_nt((q * e).astype(BF16), (k * e).astype(BF16))

    st = st_ref[...]
    o = _dot(a.astype(BF16), v) + _dot_nt((q * jnp.exp(b)).astype(BF16), st.astype(BF16))
    kd = (k * jnp.exp(b_tot - b)).astype(BF16)
    st_ref[...] = st * jnp.exp(b_tot) + _dot_tn(v, kd)
    return o


def _gla_kernel(*refs, zero_init):
    (qf_ref, kf_ref, vf_ref, zf_ref, qb_ref, kb_ref, vb_ref, zb_ref,
     wbf_ref, baf_ref, wbb_ref, bab_ref, mf_ref, maskf_ref, mb_ref, maskb_ref) = refs[:16]
    if zero_init:
        of_ref, ob_ref, sfo_ref, sbo_ref, stf_ref, stb_ref = refs[16:]
    else:
        sfi_ref, sbi_ref, of_ref, ob_ref, sfo_ref, sbo_ref, stf_ref, stb_ref = refs[16:]
    ci = pl.program_id(2)
    c = qf_ref.shape[0]

    @pl.when(ci == 0)
    def _():
        if zero_init:
            stf_ref[...] = jnp.zeros_like(stf_ref)
            stb_ref[...] = jnp.zeros_like(stb_ref)
        else:
            stf_ref[...] = sfi_ref[...].T
            stb_ref[...] = sbi_ref[...].T

    of_ref[...] = _gla_direction(qf_ref, kf_ref, vf_ref, zf_ref[:, 0:GATE_RANK], wbf_ref, baf_ref,
                                 mf_ref, maskf_ref, stf_ref, c - 1)
    ob_ref[...] = _gla_direction(qb_ref, kb_ref, vb_ref, zb_ref[:, GATE_RANK:2 * GATE_RANK], wbb_ref, bab_ref,
                                 mb_ref, maskb_ref, stb_ref, 0)

    @pl.when(ci == pl.num_programs(2) - 1)
    def _():
        sfo_ref[...] = stf_ref[...].T
        sbo_ref[...] = stb_ref[...].T


def _gla(y, wb_f, ba_f, wb_b, ba_b, s_f, s_b, *, n_seq, s_len, row0, name):
    c = GLA_CHUNK
    n = s_len // c
    r0 = row0 // c
    zero_init = s_f is None
    m_f, mask_f = _gla_consts(c, False)
    m_b, mask_b = _gla_consts(c, True)
    fwd = lambda b, h, i: r0 + b * n + i
    bwd = lambda b, h, i: r0 + b * n + (n - 1 - i)
    v_blk0 = 2 * HK_C // DV_C
    z_blk = (2 * HK_C + 2 * HV_C) // 128

    def dir_specs(rows):
        return [pl.BlockSpec((c, DK_C), lambda b, h, i: (rows(b, h, i), h)),
                pl.BlockSpec((c, DK_C), lambda b, h, i: (rows(b, h, i), H_C + h)),
                pl.BlockSpec((c, DV_C), lambda b, h, i: (rows(b, h, i), v_blk0 + h)),
                pl.BlockSpec((c, 128), lambda b, h, i: (rows(b, h, i), z_blk))]

    head_w = pl.BlockSpec((GATE_RANK, DK_C), lambda b, h, i: (0, h))
    head_b = pl.BlockSpec((1, DK_C), lambda b, h, i: (0, h))
    const2 = lambda a: pl.BlockSpec(a.shape, lambda b, h, i: (0, 0))
    const3 = lambda a: pl.BlockSpec(a.shape, lambda b, h, i: (0, 0, 0))
    state = pl.BlockSpec((None, None, DK_C, DV_C), lambda b, h, i: (b, h, 0, 0))
    in_specs = (dir_specs(fwd) + dir_specs(bwd)
                + [head_w, head_b, head_w, head_b, const2(m_f), const3(mask_f), const2(m_b), const3(mask_b)])
    args = [y] * 8 + [wb_f, ba_f.reshape(1, -1), wb_b, ba_b.reshape(1, -1), m_f, mask_f, m_b, mask_b]
    if not zero_init:
        in_specs += [state, state]
        args += [s_f, s_b]
    rows_out = n_seq * s_len
    return pl.pallas_call(
        functools.partial(_gla_kernel, zero_init=zero_init),
        grid=(n_seq, H_C, n),
        in_specs=in_specs,
        out_specs=[pl.BlockSpec((c, DV_C), lambda b, h, i: (b * n + i, h)),
                   pl.BlockSpec((c, DV_C), lambda b, h, i: (b * n + (n - 1 - i), h)),
                   state, state],
        out_shape=[jax.ShapeDtypeStruct((rows_out, HV_C), F32),
                   jax.ShapeDtypeStruct((rows_out, HV_C), F32),
                   jax.ShapeDtypeStruct((n_seq, H_C, DK_C, DV_C), F32),
                   jax.ShapeDtypeStruct((n_seq, H_C, DK_C, DV_C), F32)],
        scratch_shapes=[pltpu.VMEM((DV_C, DK_C), F32), pltpu.VMEM((DV_C, DK_C), F32)],
        compiler_params=_params("arbitrary", "arbitrary", "arbitrary"),
        name=name,
    )(*args)


def _gla_out_kernel(of_ref, ob_ref, r_ref, go_ref, w_ref, x_ref, gpost_ref, gate_ref,
                    gn_ref, sh_ref, sc_ref, xo_ref, ho_ref, acc_ref):
    k = pl.program_id(1)

    @pl.when(k == 0)
    def _():
        acc_ref[...] = jnp.zeros_like(acc_ref)

    r = r_ref[...]
    a = _rms(of_ref[...] + ob_ref[...], go_ref[...]) * (r * (1.0 / (1.0 + jnp.exp(-r))))
    acc_ref[...] += _dot(a.astype(BF16), w_ref[...].astype(BF16))

    @pl.when(k == pl.num_programs(1) - 1)
    def _():
        _residual_epilogue(acc_ref[...], x_ref, gpost_ref, gate_ref, (gn_ref, sh_ref, sc_ref), xo_ref, ho_ref)


def _gla_out(o_f, o_b, y, g_out, w, x, mod, layer, g_post, nxt):
    tm, tk = 512, DV_C
    r_blk0 = (2 * HK_C + HV_C) // tk
    specs, args, out_specs, out_shape = _res_specs(tm, mod, layer, 2, g_post, nxt)
    return pl.pallas_call(
        _gla_out_kernel,
        grid=(M_ROWS // tm, H_C),
        in_specs=[pl.BlockSpec((tm, tk), lambda i, k: (i, k)),
                  pl.BlockSpec((tm, tk), lambda i, k: (i, k)),
                  pl.BlockSpec((tm, tk), lambda i, k: (i, r_blk0 + k)),
                  pl.BlockSpec((1, tk), lambda i, k: (0, 0)),
                  pl.BlockSpec((tk, D), lambda i, k: (k, 0))] + specs,
        out_specs=out_specs,
        out_shape=out_shape,
        scratch_shapes=[pltpu.VMEM((tm, D), F32)],
        compiler_params=_params("arbitrary", "arbitrary"),
        name="gla_out",
    )(o_f, o_b, y, g_out.reshape(1, -1), w, x, *args)


def _with_cache(cache, new):
    w = new.shape[-1]
    cat = jnp.concatenate([cache.reshape(N_SAMPLE_SEQ, PAST, w).astype(BF16),
                           new.reshape(N_SAMPLE_SEQ, SAMPLE_LEN, w).astype(BF16)], axis=1)
    return cat.reshape(N_SAMPLE_SEQ * (PAST + SAMPLE_LEN), w)


def kernel(x_prompt, x_sample, cache_mla_ckv, cache_mla_krope, cache_gqa_k, cache_gqa_v, state_gla_fwd, state_gla_bwd, c, c_ctx, w_mod, b_mod, g_pre_mix, g_post_mix, g_pre_ffn, g_post_ffn, w_in_ab, g_mla_q, g_mla_kv, w_mla_uq, w_mla_ukv, g_gqa_q, g_gqa_k, w_out_ab, w_in_c, w_gate_fwd_a, w_gate_fwd_b, b_gate_fwd, w_gate_bwd_a, w_gate_bwd_b, b_gate_bwd, g_gla_out, w_out_c, w_ffn_up, ffn_conv_w, ffn_conv_b, w_ffn_down):
    depth = w_mod.shape[0]
    x = jnp.concatenate([x_prompt.reshape(NP, D), x_sample.reshape(NS, D)], axis=0)
    cond = jnp.concatenate([c_ctx[None], c, jnp.zeros((N_MOD_ROWS - 1 - N_SAMPLE_SEQ, D), F32)], axis=0)
    mod = _modulation(cond, w_mod, b_mod).reshape(depth * N_MOD_ROWS * 6, 1, D)

    h = _prenorm(x, g_pre_mix[0], mod, 0)
    caches, states = [], []
    for i in range(depth):
        j = i // 2
        ffn_pre = (g_pre_ffn[i], i, 3, 4)
        if i % 2 == 0:
            w = w_in_ab[j]
            w_in = jnp.concatenate(
                [w[:, 0:768], w[:, 832:2368], w[:, 768:832], jnp.zeros((D, IN_AB_PAD - 2368), F32)], axis=1)
            wq = w_mla_uq[j].reshape(Q_LORA, H_A, NOPE_A + ROPE_A)
            w_uq = jnp.concatenate([wq, jnp.zeros((Q_LORA, H_A, 256 - NOPE_A - ROPE_A), F32)],
                                   axis=2).reshape(Q_LORA, H_A * 256)
            y = _matmul(h, w_in, tm=1024, tn=512, out_dtype=F32, name="in_ab")
            qa, qb, ckv, kr, kb, vb = _ab_post(y, g_mla_q[j], g_mla_kv[j], g_gqa_q[j], g_gqa_k[j], w_uq)
            ckv_all = jnp.concatenate([ckv[:NP].astype(BF16), _with_cache(cache_mla_ckv[:, j], ckv[NP:])], axis=0)
            kv = _matmul(ckv_all, w_mla_ukv[j], tm=1024, tn=1024, out_dtype=BF16, name="kv_up")
            o_p = _attention(qa, qb, kv[:NP], kr[:NP].astype(BF16), kb[:NP].astype(BF16), vb[:NP].astype(BF16),
                             n_seq=N_PROMPT_SEQ, s_len=PROMPT_LEN, t_len=PROMPT_LEN, q_row0=0, name="attn_prompt")
            kr_cache = jnp.pad(cache_mla_krope[:, j], ((0, 0), (0, 0), (0, 128 - ROPE_A)))
            o_s = _attention(qa, qb, kv[NP:], _with_cache(kr_cache, kr[NP:]),
                             _with_cache(cache_gqa_k[:, j], kb[NP:]), _with_cache(cache_gqa_v[:, j], vb[NP:]),
                             n_seq=N_SAMPLE_SEQ, s_len=SAMPLE_LEN, t_len=PAST + SAMPLE_LEN, q_row0=NP,
                             name="attn_sample")
            o = jnp.concatenate([o_p, o_s], axis=0)
            x, h = _proj_res(o, w_out_ab[j], x, mod, i, 2, g_post_mix[i], ffn_pre, name="out_ab")
            caches.append((ckv[:NP].reshape(N_PROMPT_SEQ, PROMPT_LEN, KV_LORA),
                           kr[:NP, :ROPE_A].reshape(N_PROMPT_SEQ, PROMPT_LEN, ROPE_A),
                           kb[:NP].reshape(N_PROMPT_SEQ, PROMPT_LEN, KV_B, HD_B),
                           vb[:NP].reshape(N_PROMPT_SEQ, PROMPT_LEN, KV_B, HD_B)))
        else:
            w_in = jnp.concatenate([w_in_c[j], w_gate_fwd_a[j], w_gate_bwd_a[j],
                                    jnp.zeros((D, 128 - 2 * GATE_RANK), F32)], axis=1)
            y = _matmul(h, w_in, tm=1024, tn=896, out_dtype=F32, name="in_c")
            gate_args = (w_gate_fwd_b[j], b_gate_fwd[j], w_gate_bwd_b[j], b_gate_bwd[j])
            of_p, ob_p, sf, sb = _gla(y, *gate_args, None, None,
                                      n_seq=N_PROMPT_SEQ, s_len=PROMPT_LEN, row0=0, name="gla_prompt")
            of_s, ob_s, _, _ = _gla(y, *gate_args, state_gla_fwd[:, j], state_gla_bwd[:, j],
                                    n_seq=N_SAMPLE_SEQ, s_len=SAMPLE_LEN, row0=NP, name="gla_sample")
            o_f = jnp.concatenate([of_p, of_s], axis=0)
            o_b = jnp.concatenate([ob_p, ob_s], axis=0)
            x, h = _gla_out(o_f, o_b, y, g_gla_out[j], w_out_c[j], x, mod, i, g_post_mix[i], ffn_pre)
            states.append((sf, sb))
        act = _ffn_up(h, w_ffn_up[i], ffn_conv_w[i], ffn_conv_b[i])
        nxt = (g_pre_mix[i + 1], i + 1, 0, 1) if i + 1 < depth else None
        x, h = _proj_res(act, w_ffn_down[i], x, mod, i, 5, g_post_ffn[i], nxt, name="ffn_down")

    y_prompt = x[:NP].reshape(N_PROMPT_SEQ, PROMPT_LEN, D)
    y_sample = x[NP:].reshape(N_SAMPLE_SEQ, SAMPLE_LEN, D)
    new_ckv, new_krope, new_k, new_v = (jnp.stack(t, axis=1) for t in zip(*caches))
    new_sf, new_sb = (jnp.stack(t, axis=1) for t in zip(*states))
    return (y_prompt, y_sample, new_ckv, new_krope, new_k, new_v, new_sf, new_sb)
```

```python
import functools
import math

import numpy as np
import jax
import jax.numpy as jnp
from jax import lax
from jax.experimental import pallas as pl
from jax.experimental.pallas import tpu as pltpu

D = 2048
N_PROMPT_SEQ, PROMPT_LEN = 16, 256
N_SAMPLE_SEQ, SAMPLE_LEN = 4, 2048
PAST = 512
GRID_W = 64
NP = N_PROMPT_SEQ * PROMPT_LEN
NS = N_SAMPLE_SEQ * SAMPLE_LEN
M_ROWS = NP + NS
EPS = 1e-6
ROPE_THETA = 10000.0

H_A, NOPE_A, ROPE_A, V_A = 8, 128, 64, 128
Q_LORA, KV_LORA = 512, 256
H_B, KV_B, HD_B = 8, 2, 128
IN_AB_PAD = 2560
H_C, DK_C, DV_C = 4, 256, 512
HK_C, HV_C = H_C * DK_C, H_C * DV_C
GATE_RANK = 16
GATE_TAU = 16.0
D_FF = 5632
GLA_CHUNK = 128
N_MOD_ROWS = 8

VMEM_LIMIT_BYTES = 56 * 1024 * 1024
BF16 = jnp.bfloat16
F32 = jnp.float32


def _params(*sem):
    return pltpu.CompilerParams(dimension_semantics=sem, vmem_limit_bytes=VMEM_LIMIT_BYTES)


def _mod_row(i, tm):
    n_prompt_tiles = NP // tm
    per_seq = SAMPLE_LEN // tm
    return jnp.where(i < n_prompt_tiles, 0, 1 + (i - n_prompt_tiles) // per_seq)


def _mod_spec(layer, piece, tm, tile0=0):
    def index_map(i, *_):
        return ((layer * N_MOD_ROWS + _mod_row(tile0 + i, tm)) * 6 + piece, 0, 0)
    return pl.BlockSpec((None, 1, D), index_map)


def _rms(x, g):
    return x * lax.rsqrt(jnp.mean(x * x, axis=-1, keepdims=True) + EPS) * g


def _dot(a, b):
    return jnp.dot(a, b, preferred_element_type=F32)


def _dot_nt(a, b):
    return lax.dot_general(a, b, (((1,), (1,)), ((), ())), preferred_element_type=F32)


def _split2(x):
    hi = x.astype(BF16)
    lo = (x - hi.astype(F32)).astype(BF16)
    return hi, lo


def _mod_kernel(c_ref, w_ref, b_ref, o_ref):
    c = c_ref[...]
    a = c * (1.0 / (1.0 + jnp.exp(-c)))
    a_hi, a_lo = _split2(a)
    w_hi, w_lo = _split2(w_ref[...])
    r = _dot(jnp.concatenate([a_hi, a_lo], axis=0), w_hi)
    o_ref[...] = r[:N_MOD_ROWS] + r[N_MOD_ROWS:] + _dot(a_hi, w_lo) + b_ref[...]


def _modulation(cond, w_mod, b_mod):
    depth = w_mod.shape[0]
    n = w_mod.shape[2]
    tn = 1024
    return pl.pallas_call(
        _mod_kernel,
        grid=(depth, n // tn),
        in_specs=[pl.BlockSpec((N_MOD_ROWS, D), lambda l, j: (0, 0)),
                  pl.BlockSpec((None, D, tn), lambda l, j: (l, 0, j)),
                  pl.BlockSpec((None, 1, tn), lambda l, j: (l, 0, j))],
        out_specs=pl.BlockSpec((None, N_MOD_ROWS, tn), lambda l, j: (l, 0, j)),
        out_shape=jax.ShapeDtypeStruct((depth, N_MOD_ROWS, n), F32),
        compiler_params=_params("arbitrary", "arbitrary"),
        name="modulation",
    )(cond, w_mod, b_mod.reshape(depth, 1, n))


def _prenorm_kernel(xp_ref, xs_ref, g_ref, sh_ref, sc_ref, x_ref, h_ref, *, n_prompt_tiles):
    def emit(src_ref):
        x = src_ref[...]
        x_ref[...] = x
        h_ref[...] = (_rms(x, g_ref[...]) * (1.0 + sc_ref[...]) + sh_ref[...]).astype(h_ref.dtype)

    i = pl.program_id(0)
    pl.when(i < n_prompt_tiles)(lambda: emit(xp_ref))
    pl.when(i >= n_prompt_tiles)(lambda: emit(xs_ref))


def _prenorm(x_prompt, x_sample, g, mod, layer):
    tm = 512
    npt = NP // tm
    row = pl.BlockSpec((tm, D), lambda i: (i, 0))
    return pl.pallas_call(
        functools.partial(_prenorm_kernel, n_prompt_tiles=npt),
        grid=(M_ROWS // tm,),
        in_specs=[pl.BlockSpec((tm, D), lambda i: (jnp.minimum(i, npt - 1), 0)),
                  pl.BlockSpec((tm, D), lambda i: (jnp.maximum(i - npt, 0), 0)),
                  pl.BlockSpec((1, D), lambda i: (0, 0)),
                  _mod_spec(layer, 0, tm), _mod_spec(layer, 1, tm)],
        out_specs=[row, row],
        out_shape=[jax.ShapeDtypeStruct((M_ROWS, D), F32), jax.ShapeDtypeStruct((M_ROWS, D), BF16)],
        compiler_params=_params("arbitrary"),
        name="prenorm",
    )(x_prompt.reshape(NP, D), x_sample.reshape(NS, D), g.reshape(1, D), mod, mod)


def _matmul_kernel(x_ref, w_ref, o_ref):
    o_ref[...] = _dot(x_ref[...].astype(BF16), w_ref[...].astype(BF16)).astype(o_ref.dtype)


def _matmul(x, w, *, tm, tn, out_dtype, name):
    m, k = x.shape
    n = w.shape[1]
    return pl.pallas_call(
        _matmul_kernel,
        grid=(n // tn, m // tm),
        in_specs=[pl.BlockSpec((tm, k), lambda j, i: (i, 0)),
                  pl.BlockSpec((k, tn), lambda j, i: (0, j))],
        out_specs=pl.BlockSpec((tm, tn), lambda j, i: (i, j)),
        out_shape=jax.ShapeDtypeStruct((m, n), out_dtype),
        compiler_params=_params("arbitrary", "arbitrary"),
        name=name,
    )(x, w)


def _rope_tables(rot_dim, tm):
    q = rot_dim // 4
    inv = ROPE_THETA ** (-np.arange(q, dtype=np.float64) / q)
    tok = np.arange(SAMPLE_LEN)
    ang_r = (tok // GRID_W)[:, None] * inv
    ang_c = (tok % GRID_W)[:, None] * inv
    ang = np.concatenate([ang_r, ang_r, ang_c, ang_c], axis=-1)
    cos, sin = np.cos(ang), np.sin(ang)
    first = (np.arange(rot_dim) % (2 * q)) < q
    sin_m = np.where(first, -sin, 0.0)
    sin_p = np.where(first, 0.0, sin)

    def widen(t, ident):
        full = np.full((tm + SAMPLE_LEN, 128), ident, np.float64)
        full[tm:, :rot_dim] = t
        return jnp.asarray(full, F32)

    return widen(cos, 1.0), widen(sin_m, 0.0), widen(sin_p, 0.0), q


def _rope128(x, cos, sin_m, sin_p, q):
    return (x * cos + pltpu.roll(x, 128 - q, 1) * sin_m + pltpu.roll(x, q, 1) * sin_p)


def _ab_post_kernel(y_ref, gq_ref, gkv_ref, gqn_ref, gkn_ref, wuq_ref,
                    ca_ref, ma_ref, pa_ref, cb_ref, mb_ref, pb_ref,
                    qa_ref, qb_ref, ckv_ref, kr_ref, kb_ref, vb_ref, *, qa_rot, qb_rot):
    ca, ma, pa = ca_ref[...], ma_ref[...], pa_ref[...]
    cb, mb, pb = cb_ref[...], mb_ref[...], pb_ref[...]
    scale_a = (NOPE_A + ROPE_A) ** -0.5
    scale_b = HD_B ** -0.5

    cq = _rms(y_ref[:, 0:Q_LORA], gq_ref[...])
    qa = _dot(cq.astype(BF16), wuq_ref[...].astype(BF16)) * scale_a
    for h in range(H_A):
        base = h * 256
        qa_ref[:, base:base + 128] = qa[:, base:base + 128].astype(qa_ref.dtype)
        qa_ref[:, base + 128:base + 256] = _rope128(
            qa[:, base + 128:base + 256], ca, ma, pa, qa_rot).astype(qa_ref.dtype)

    ckv_ref[...] = _rms(y_ref[:, 512:768], gkv_ref[...])
    kr_ref[...] = _rope128(y_ref[:, 2304:2432], ca, ma, pa, qa_rot)

    for h in range(H_B):
        xb = _rms(y_ref[:, 768 + h * 128:768 + (h + 1) * 128], gqn_ref[...])
        qb_ref[:, h * 128:(h + 1) * 128] = (
            _rope128(xb, cb, mb, pb, qb_rot) * scale_b).astype(qb_ref.dtype)
    for g in range(KV_B):
        xk = _rms(y_ref[:, 1792 + g * 128:1792 + (g + 1) * 128], gkn_ref[...])
        kb_ref[:, g * 128:(g + 1) * 128] = _rope128(xk, cb, mb, pb, qb_rot)
    vb_ref[...] = y_ref[:, 2048:2304]


def _ab_post(y, g_q, g_kv, g_qn, g_kn, w_uq_pad):
    tm = 256
    ca, ma, pa, qa_rot = _rope_tables(ROPE_A, tm)
    cb, mb, pb, qb_rot = _rope_tables(HD_B, tm)
    n_prompt_tiles = NP // tm
    per_seq = SAMPLE_LEN // tm

    def tab_map(i):
        return (jnp.where(i < n_prompt_tiles, 0, 1 + (i - n_prompt_tiles) % per_seq), 0)

    tab = pl.BlockSpec((tm, 128), tab_map)
    row = lambda w: pl.BlockSpec((tm, w), lambda i: (i, 0))
    vec = lambda w: pl.BlockSpec((1, w), lambda i: (0, 0))
    return pl.pallas_call(
        functools.partial(_ab_post_kernel, qa_rot=qa_rot, qb_rot=qb_rot),
        grid=(M_ROWS // tm,),
        in_specs=[row(IN_AB_PAD), vec(Q_LORA), vec(KV_LORA), vec(HD_B), vec(HD_B),
                  pl.BlockSpec((Q_LORA, H_A * 256), lambda i: (0, 0)),
                  tab, tab, tab, tab, tab, tab],
        out_specs=[row(H_A * 256), row(H_B * HD_B), row(KV_LORA), row(128),
                   row(KV_B * HD_B), row(KV_B * HD_B)],
        out_shape=[jax.ShapeDtypeStruct((M_ROWS, H_A * 256), BF16),
                   jax.ShapeDtypeStruct((M_ROWS, H_B * HD_B), BF16),
                   jax.ShapeDtypeStruct((M_ROWS, KV_LORA), F32),
                   jax.ShapeDtypeStruct((M_ROWS, 128), F32),
                   jax.ShapeDtypeStruct((M_ROWS, KV_B * HD_B), F32),
                   jax.ShapeDtypeStruct((M_ROWS, KV_B * HD_B), F32)],
        compiler_params=_params("arbitrary"),
        name="ab_post",
    )(y, g_q.reshape(1, -1), g_kv.reshape(1, -1), g_qn.reshape(1, -1), g_kn.reshape(1, -1),
      w_uq_pad, ca, ma, pa, cb, mb, pb)


def _softmax_pv(s, v):
    m = jnp.max(s, axis=-1, keepdims=True)
    e = jnp.exp(s - m)
    l = jnp.sum(e, axis=-1, keepdims=True)
    return _dot(e.astype(BF16), v) / l


def _attn_kernel(qa_ref, qb_ref, kv_ref, kr_ref, kb_ref, vb_ref, *rest):
    o_ref = rest[-1]
    kr = kr_ref[...]
    for h in range(H_A):
        k = jnp.concatenate([kv_ref[:, h * 256:h * 256 + 128], kr], axis=1)
        s = _dot_nt(qa_ref[:, h * 256:(h + 1) * 256], k)
        o = _softmax_pv(s, kv_ref[:, h * 256 + 128:(h + 1) * 256])
        o_ref[:, h * 128:(h + 1) * 128] = o.astype(o_ref.dtype)
    rep = H_B // KV_B
    for h in range(H_B):
        g = h // rep
        s = _dot_nt(qb_ref[:, h * 128:(h + 1) * 128], kb_ref[:, g * 128:(g + 1) * 128])
        o = _softmax_pv(s, vb_ref[:, g * 128:(g + 1) * 128])
        o_ref[:, (H_A + h) * 128:(H_A + h + 1) * 128] = o.astype(o_ref.dtype)


def _attention(qa, qb, kv, kr, kb, vb, o_prev, *, n_seq, s_len, t_len, row0, kv_row0, name):
    tq = 256
    nq = s_len // tq
    q0 = row0 // tq
    kv0 = kv_row0 // t_len
    qspec = lambda w: pl.BlockSpec((tq, w), lambda b, i: (q0 + b * nq + i, 0))
    kspec = lambda w, b0=0: pl.BlockSpec((t_len, w), lambda b, i: (b0 + b, 0), pipeline_mode=pl.Buffered(1))
    in_specs = [qspec(H_A * 256), qspec(H_B * HD_B), kspec(H_A * 256, kv0), kspec(128),
                kspec(KV_B * HD_B), kspec(KV_B * HD_B)]
    args = [qa, qb, kv, kr, kb, vb]
    aliases = {}
    if o_prev is not None:
        in_specs.append(pl.BlockSpec(memory_space=pl.ANY))
        args.append(o_prev)
        aliases = {len(args) - 1: 0}
    return pl.pallas_call(
        _attn_kernel,
        grid=(n_seq, nq),
        in_specs=in_specs,
        out_specs=pl.BlockSpec((tq, D), lambda b, i: (q0 + b * nq + i, 0)),
        out_shape=jax.ShapeDtypeStruct((M_ROWS, D), BF16),
        input_output_aliases=aliases,
        compiler_params=_params("arbitrary", "arbitrary"),
        name=name,
    )(*args)


def _residual_epilogue(acc, x_ref, gpost_ref, gate_ref, nxt, xo_ref, ho_ref):
    x_new = x_ref[...] + gate_ref[...] * _rms(acc, gpost_ref[...])
    xo_ref[...] = x_new
    if nxt is not None:
        gn_ref, sh_ref, sc_ref = nxt
        ho_ref[...] = (_rms(x_new, gn_ref[...]) * (1.0 + sc_ref[...]) + sh_ref[...]).astype(ho_ref.dtype)


def _proj_res_kernel(*refs, emit_h):
    if emit_h:
        a_ref, w_ref, x_ref, gpost_ref, gate_ref, gn_ref, sh_ref, sc_ref, xo_ref, ho_ref, acc_ref = refs
        nxt = (gn_ref, sh_ref, sc_ref)
    else:
        a_ref, w_ref, x_ref, gpost_ref, gate_ref, xo_ref, acc_ref = refs
        nxt, ho_ref = None, None
    k = pl.program_id(1)

    @pl.when(k == 0)
    def _():
        acc_ref[...] = jnp.zeros_like(acc_ref)

    acc_ref[...] += _dot(a_ref[...], w_ref[...])

    @pl.when(k == pl.num_programs(1) - 1)
    def _():
        _residual_epilogue(acc_ref[...], x_ref, gpost_ref, gate_ref, nxt, xo_ref, ho_ref)


def _res_specs(tm, tile0, n_rows, mod, layer, gate_piece, g_post, nxt):
    vec = pl.BlockSpec((1, D), lambda i, k: (0, 0))
    specs = [pl.BlockSpec((tm, D), lambda i, k: (tile0 + i, 0)), vec, _mod_spec(layer, gate_piece, tm, tile0)]
    args = [g_post.reshape(1, D), mod]
    if nxt is not None:
        g_next, layer_next, sh_piece, sc_piece = nxt
        specs += [vec, _mod_spec(layer_next, sh_piece, tm, tile0), _mod_spec(layer_next, sc_piece, tm, tile0)]
        args += [g_next.reshape(1, D), mod, mod]
    out_specs = [pl.BlockSpec((tm, D), lambda i, k: (i, 0))]
    out_shape = [jax.ShapeDtypeStruct((n_rows, D), F32)]
    if nxt is not None:
        out_specs.append(pl.BlockSpec((tm, D), lambda i, k: (i, 0)))
        out_shape.append(jax.ShapeDtypeStruct((n_rows, D), BF16))
    return specs, args, out_specs, out_shape


def _proj_res(a, w, w_layer, x, mod, layer, gate_piece, g_post, nxt, *, tk, name, row0=0, n_rows=M_ROWS):
    tm = 512
    tile0 = row0 // tm
    kdim = a.shape[1]
    specs, args, out_specs, out_shape = _res_specs(tm, tile0, n_rows, mod, layer, gate_piece, g_post, nxt)
    out = pl.pallas_call(
        functools.partial(_proj_res_kernel, emit_h=nxt is not None),
        grid=(n_rows // tm, kdim // tk),
        in_specs=[pl.BlockSpec((tm, tk), lambda i, k: (tile0 + i, k)),
                  pl.BlockSpec((None, tk, D), lambda i, k: (w_layer, k, 0))] + specs,
        out_specs=out_specs,
        out_shape=out_shape,
        scratch_shapes=[pltpu.VMEM((tm, D), F32)],
        compiler_params=_params("arbitrary", "arbitrary"),
        name=name,
    )(a, w, x, *args)
    return out if nxt is not None else (out[0], None)


def _ffn_up_kernel(h_ref, hp_ref, hn_ref, wa_ref, wg_ref, cwa_ref, cwg_ref, cba_ref, cbg_ref,
                   mp_ref, mn_ref, o_ref):
    tm = h_ref.shape[0]
    h = h_ref[...]
    hp = hp_ref[...]
    hn = hn_ref[...]
    mp = mp_ref[...]
    mn = mn_ref[...]
    row = lax.broadcasted_iota(jnp.int32, (tm, 1), 0)
    halo = hp_ref.shape[0]

    def branch(w_ref, cw_ref, cb_ref):
        w = w_ref[...].astype(BF16)
        u = _dot(h, w)
        u_first = _dot(hp, w)[halo - 1:halo]
        u_last = _dot(hn, w)[0:1]
        prev = jnp.where(row == 0, u_first, pltpu.roll(u, 1, 0)) * mp
        nxt = jnp.where(row == tm - 1, u_last, pltpu.roll(u, tm - 1, 0)) * mn
        cw = cw_ref[...]
        return prev * cw[0:1] + u * cw[1:2] + nxt * cw[2:3] + cb_ref[...]

    a = branch(wa_ref, cwa_ref, cba_ref)
    g = branch(wg_ref, cwg_ref, cbg_ref)
    o_ref[...] = (g * (1.0 / (1.0 + jnp.exp(-g))) * a).astype(o_ref.dtype)


def _seq_edge_masks():
    pos = np.concatenate([np.tile(np.arange(PROMPT_LEN), N_PROMPT_SEQ),
                          np.tile(np.arange(SAMPLE_LEN), N_SAMPLE_SEQ)])
    lens = np.concatenate([np.full(NP, PROMPT_LEN), np.full(NS, SAMPLE_LEN)])
    has_prev = (pos > 0).astype(np.float32).reshape(M_ROWS, 1)
    has_next = (pos < lens - 1).astype(np.float32).reshape(M_ROWS, 1)
    return jnp.asarray(has_prev), jnp.asarray(has_next)


def _ffn_up(h, w_up, conv_w, conv_b, layer):
    tm, tn, halo = 1024, 512, 16
    nj = D_FF // tn
    r = tm // halo
    last = M_ROWS // halo - 1
    has_prev, has_next = _seq_edge_masks()
    conv_b = conv_b.reshape(conv_b.shape[0], 1, 2 * D_FF)
    col = lambda rows, half: pl.BlockSpec((None, rows, tn), lambda j, i: (layer, 0, j + half * nj))
    return pl.pallas_call(
        _ffn_up_kernel,
        grid=(nj, M_ROWS // tm),
        in_specs=[pl.BlockSpec((tm, D), lambda j, i: (i, 0)),
                  pl.BlockSpec((halo, D), lambda j, i: (jnp.maximum(i * r - 1, 0), 0)),
                  pl.BlockSpec((halo, D), lambda j, i: (jnp.minimum((i + 1) * r, last), 0)),
                  col(D, 0), col(D, 1), col(3, 0), col(3, 1), col(1, 0), col(1, 1),
                  pl.BlockSpec((tm, 1), lambda j, i: (i, 0)),
                  pl.BlockSpec((tm, 1), lambda j, i: (i, 0))],
        out_specs=pl.BlockSpec((tm, tn), lambda j, i: (i, j)),
        out_shape=jax.ShapeDtypeStruct((M_ROWS, D_FF), BF16),
        compiler_params=_params("arbitrary", "arbitrary"),
        name="ffn_up",
    )(h, h, h, w_up, w_up, conv_w, conv_w, conv_b, conv_b, has_prev, has_next)


def _gla_masks(c):
    levels = int(math.log2(c))
    t = np.arange(c)
    pair = [t[:, None] == t[None, :]]
    later = []
    for l in range(1, levels + 1):
        blk = t >> l
        up = ((t >> (l - 1)) & 1) == 1
        pair.append((blk[:, None] == blk[None, :]) & up[:, None] & ~up[None, :])
        later.append(np.broadcast_to(up[:, None], (c, DK_C)))
    pair = np.stack(pair).astype(np.float32)
    pair = np.stack([pair, pair.transpose(0, 2, 1)])
    return jnp.asarray(pair), jnp.asarray(np.stack(later).astype(np.float32))


def _gla_chain(q, k, v, za, wb, ba, pair_ref, later_ref, s_ref, reverse):
    c = q.shape[0]
    levels = later_ref.shape[0]
    za_hi, za_lo = _split2(za)
    wb_hi, wb_lo = _split2(wb)
    z = _dot(za_hi, wb_hi) + _dot(za_lo, wb_hi) + _dot(za_hi, wb_lo) + ba
    x = (jnp.minimum(z, 0.0) - jnp.log(1.0 + jnp.exp(-jnp.abs(z)))) * (math.log2(math.e) / GATE_TAU)
    yield None

    rev = 1 if reverse else 0
    first_i, second_i = (1, 0) if reverse else (0, 1)
    p, t = x, x
    k_bf = k.astype(BF16)
    a = pair_ref[rev, 0] * _dot_nt(q.astype(BF16), k_bf)
    for l in range(1, levels + 1):
        s = 1 << (l - 1)
        if s < 8:
            later = later_ref[l - 1] > 0.5
            t_sib = jnp.where(later, pltpu.roll(t, s, 0), pltpu.roll(t, c - s, 0))
            if reverse:
                d = jnp.where(later, t - p, p)
                p = jnp.where(later, p, p + t_sib)
            else:
                d = jnp.where(later, p, t - p)
                p = jnp.where(later, p + t_sib, p)
            t = t + t_sib
            e = jnp.exp2(d)
            a = a + pair_ref[rev, l] * _dot_nt((q * e).astype(BF16), (k * e).astype(BF16))
        else:
            nb = c // (2 * s)
            split = lambda arr: arr.reshape(nb, 2, s, arr.shape[-1])
            join = lambda f, sec: jnp.stack([f, sec] if first_i == 0 else [sec, f], axis=1)
            p4, t4 = split(p), split(t)
            p_f, p_s, t_f, t_s = p4[:, first_i], p4[:, second_i], t4[:, first_i], t4[:, second_i]
            q_s = split(q)[:, second_i] * jnp.exp2(p_s)
            k_f = split(k)[:, first_i] * jnp.exp2(t_f - p_f)
            k_all = join(k_f.astype(BF16), split(k_bf)[:, second_i]).reshape(c, DK_C)
            r = _dot_nt(q_s.reshape(c // 2, DK_C).astype(BF16), k_all).reshape(nb, s, c)
            a4 = a.reshape(nb, 2, s, c)
            mask = pair_ref[rev, l].reshape(nb, 2, s, c)[:, second_i]
            a = join(a4[:, first_i], a4[:, second_i] + mask * r).reshape(c, c)
            t_new = t_f + t_s
            p = join(p_f, p_s + t_f).reshape(c, DK_C)
            t = join(t_new, t_new).reshape(c, DK_C)
        yield None
    s_old = s_ref[...]
    o_inter = _dot((q * jnp.exp2(p)).astype(BF16), s_old.astype(BF16))
    kd_t = (k * jnp.exp2(t - p)).T.astype(BF16)
    both = _dot(jnp.concatenate([a.astype(BF16), kd_t], axis=0), v)
    dec_col = jnp.exp2(t[0:8]).T[:, 0:1]
    s_ref[...] = s_old * dec_col + both[c:]
    yield both[:c] + o_inter


def _gla_kernel(*refs, zero_init, aliased):
    (qf_ref, kf_ref, vf_ref, zf_ref, qb_ref, kb_ref, vb_ref, zb_ref,
     wbf_ref, baf_ref, wbb_ref, bab_ref, pair_ref, later_ref) = refs[:14]
    rest = refs[14:]
    if not zero_init:
        sfi_ref, sbi_ref = rest[:2]
        rest = rest[2:]
    if aliased:
        rest = rest[2:]
    of_ref, ob_ref, sfo_ref, sbo_ref, sf_ref, sb_ref = rest
    ci = pl.program_id(1)

    @pl.when(ci == 0)
    def _():
        if zero_init:
            sf_ref[...] = jnp.zeros_like(sf_ref)
            sb_ref[...] = jnp.zeros_like(sb_ref)
        else:
            sf_ref[...] = sfi_ref[...]
            sb_ref[...] = sbi_ref[...]

    scale = DK_C ** -0.5
    chains = []
    for h in range(H_C):
        ks, vs = slice(h * DK_C, (h + 1) * DK_C), slice(h * DV_C, (h + 1) * DV_C)
        chains.append((of_ref, vs, _gla_chain(
            qf_ref[:, ks].astype(F32) * scale, kf_ref[:, ks].astype(F32), vf_ref[:, vs],
            zf_ref[:, 0:GATE_RANK], wbf_ref[:, ks], baf_ref[:, ks], pair_ref, later_ref, sf_ref.at[h], False)))
        chains.append((ob_ref, vs, _gla_chain(
            qb_ref[:, ks].astype(F32) * scale, kb_ref[:, ks].astype(F32), vb_ref[:, vs],
            zb_ref[:, GATE_RANK:2 * GATE_RANK], wbb_ref[:, ks], bab_ref[:, ks], pair_ref, later_ref,
            sb_ref.at[h], True)))
    live = True
    while live:
        for o_ref, vs, chain in chains:
            out = next(chain, "done")
            if isinstance(out, str):
                live = False
            elif out is not None:
                o_ref[:, vs] = out.astype(o_ref.dtype)

    @pl.when(ci == pl.num_programs(1) - 1)
    def _():
        sfo_ref[...] = sf_ref[...]
        sbo_ref[...] = sb_ref[...]


def _gla(y, z, wb_f, ba_f, wb_b, ba_b, s_f, s_b, o_prev, *, n_seq, s_len, row0, name):
    c = GLA_CHUNK
    n = s_len // c
    r0 = row0 // c
    zero_init = s_f is None
    pair, later = _gla_masks(c)
    fwd = lambda b, i: r0 + b * n + i
    bwd = lambda b, i: r0 + b * n + (n - 1 - i)

    def dir_specs(rows):
        return [pl.BlockSpec((c, HK_C), lambda b, i: (rows(b, i), 0)),
                pl.BlockSpec((c, HK_C), lambda b, i: (rows(b, i), 1)),
                pl.BlockSpec((c, HV_C), lambda b, i: (rows(b, i), 1)),
                pl.BlockSpec((c, 128), lambda b, i: (rows(b, i), 0))]

    full = lambda arr: pl.BlockSpec(arr.shape, lambda b, i: (0,) * arr.ndim)
    state = pl.BlockSpec((None, H_C, DK_C, DV_C), lambda b, i: (b, 0, 0, 0))
    ba_f, ba_b = ba_f.reshape(1, -1), ba_b.reshape(1, -1)
    in_specs = (dir_specs(fwd) + dir_specs(bwd)
                + [full(wb_f), full(ba_f), full(wb_b), full(ba_b), full(pair), full(later)])
    args = [y, y, y, z, y, y, y, z, wb_f, ba_f, wb_b, ba_b, pair, later]
    if not zero_init:
        in_specs += [state, state]
        args += [s_f, s_b]
    aliases = {}
    if o_prev is not None:
        in_specs += [pl.BlockSpec(memory_space=pl.ANY)] * 2
        aliases = {len(args): 0, len(args) + 1: 1}
        args += list(o_prev)
    return pl.pallas_call(
        functools.partial(_gla_kernel, zero_init=zero_init, aliased=o_prev is not None),
        grid=(n_seq, n),
        in_specs=in_specs,
        out_specs=[pl.BlockSpec((c, HV_C), lambda b, i: (fwd(b, i), 0)),
                   pl.BlockSpec((c, HV_C), lambda b, i: (bwd(b, i), 0)),
                   state, state],
        out_shape=[jax.ShapeDtypeStruct((M_ROWS, HV_C), BF16),
                   jax.ShapeDtypeStruct((M_ROWS, HV_C), BF16),
                   jax.ShapeDtypeStruct((n_seq, H_C, DK_C, DV_C), F32),
                   jax.ShapeDtypeStruct((n_seq, H_C, DK_C, DV_C), F32)],
        scratch_shapes=[pltpu.VMEM((H_C, DK_C, DV_C), F32), pltpu.VMEM((H_C, DK_C, DV_C), F32)],
        input_output_aliases=aliases,
        compiler_params=_params("arbitrary", "arbitrary"),
        name=name,
    )(*args)


def _gla_out_kernel(of_ref, ob_ref, r_ref, go_ref, w_ref, x_ref, gpost_ref, gate_ref,
                    gn_ref, sh_ref, sc_ref, xo_ref, ho_ref, acc_ref):
    k = pl.program_id(1)

    @pl.when(k == 0)
    def _():
        acc_ref[...] = jnp.zeros_like(acc_ref)

    r = r_ref[...].astype(F32)
    o = of_ref[...].astype(F32) + ob_ref[...].astype(F32)
    a = _rms(o, go_ref[...]) * (r * (1.0 / (1.0 + jnp.exp(-r))))
    acc_ref[...] += _dot(a.astype(BF16), w_ref[...])

    @pl.when(k == pl.num_programs(1) - 1)
    def _():
        _residual_epilogue(acc_ref[...], x_ref, gpost_ref, gate_ref, (gn_ref, sh_ref, sc_ref), xo_ref, ho_ref)


def _gla_out(o_f, o_b, y, g_out, w, w_layer, x, mod, layer, g_post, nxt):
    tm, tk = 512, DV_C
    r_blk0 = (2 * HK_C + HV_C) // tk
    specs, args, out_specs, out_shape = _res_specs(tm, 0, M_ROWS, mod, layer, 2, g_post, nxt)
    return pl.pallas_call(
        _gla_out_kernel,
        grid=(M_ROWS // tm, H_C),
        in_specs=[pl.BlockSpec((tm, tk), lambda i, k: (i, k)),
                  pl.BlockSpec((tm, tk), lambda i, k: (i, k)),
                  pl.BlockSpec((tm, tk), lambda i, k: (i, r_blk0 + k)),
                  pl.BlockSpec((1, tk), lambda i, k: (0, 0)),
                  pl.BlockSpec((None, tk, D), lambda i, k: (w_layer, k, 0))] + specs,
        out_specs=out_specs,
        out_shape=out_shape,
        scratch_shapes=[pltpu.VMEM((tm, D), F32)],
        compiler_params=_params("arbitrary", "arbitrary"),
        name="gla_out",
    )(o_f, o_b, y, g_out.reshape(1, -1), w, x, *args)


def _with_cache(cache, new):
    w = new.shape[-1]
    cat = jnp.concatenate([cache.reshape(N_SAMPLE_SEQ, PAST, w).astype(BF16),
                           new.reshape(N_SAMPLE_SEQ, SAMPLE_LEN, w).astype(BF16)], axis=1)
    return cat.reshape(N_SAMPLE_SEQ * (PAST + SAMPLE_LEN), w)


def kernel(x_prompt, x_sample, cache_mla_ckv, cache_mla_krope, cache_gqa_k, cache_gqa_v, state_gla_fwd, state_gla_bwd, c, c_ctx, w_mod, b_mod, g_pre_mix, g_post_mix, g_pre_ffn, g_post_ffn, w_in_ab, g_mla_q, g_mla_kv, w_mla_uq, w_mla_ukv, g_gqa_q, g_gqa_k, w_out_ab, w_in_c, w_gate_fwd_a, w_gate_fwd_b, b_gate_fwd, w_gate_bwd_a, w_gate_bwd_b, b_gate_bwd, g_gla_out, w_out_c, w_ffn_up, ffn_conv_w, ffn_conv_b, w_ffn_down):
    depth = w_mod.shape[0]
    cond = jnp.concatenate([c_ctx[None], c, jnp.zeros((N_MOD_ROWS - 1 - N_SAMPLE_SEQ, D), F32)], axis=0)
    mod = _modulation(cond, w_mod, b_mod).reshape(depth * N_MOD_ROWS * 6, 1, D)
    w_out_ab_bf, w_out_c_bf, w_down_bf = w_out_ab.astype(BF16), w_out_c.astype(BF16), w_ffn_down.astype(BF16)
    t_sample = PAST + SAMPLE_LEN

    x, h = _prenorm(x_prompt, x_sample, g_pre_mix[0], mod, 0)
    caches, states = [], []
    for i in range(depth):
        j = i // 2
        ffn_pre = (g_pre_ffn[i], i, 3, 4)
        if i % 2 == 0:
            w = w_in_ab[j]
            w_in = jnp.concatenate(
                [w[:, 0:768], w[:, 832:2368], w[:, 768:832], jnp.zeros((D, IN_AB_PAD - 2368), F32)], axis=1)
            wq = w_mla_uq[j].reshape(Q_LORA, H_A, NOPE_A + ROPE_A)
            w_uq = jnp.concatenate([wq, jnp.zeros((Q_LORA, H_A, 256 - NOPE_A - ROPE_A), F32)],
                                   axis=2).reshape(Q_LORA, H_A * 256)
            y = _matmul(h, w_in, tm=1024, tn=512, out_dtype=F32, name="in_ab")
            qa, qb, ckv, kr, kb, vb = _ab_post(y, g_mla_q[j], g_mla_kv[j], g_gqa_q[j], g_gqa_k[j], w_uq)
            ckv_all = jnp.concatenate([_with_cache(cache_mla_ckv[:, j], ckv[NP:]), ckv[:NP].astype(BF16)], axis=0)
            kv = _matmul(ckv_all, w_mla_ukv[j], tm=1024, tn=1024, out_dtype=BF16, name="kv_up")
            o = _attention(qa, qb, kv, kr[:NP].astype(BF16), kb[:NP].astype(BF16), vb[:NP].astype(BF16), None,
                           n_seq=N_PROMPT_SEQ, s_len=PROMPT_LEN, t_len=PROMPT_LEN, row0=0,
                           kv_row0=N_SAMPLE_SEQ * t_sample, name="attn_prompt")
            kr_cache = jnp.pad(cache_mla_krope[:, j], ((0, 0), (0, 0), (0, 128 - ROPE_A)))
            o = _attention(qa, qb, kv, _with_cache(kr_cache, kr[NP:]),
                           _with_cache(cache_gqa_k[:, j], kb[NP:]), _with_cache(cache_gqa_v[:, j], vb[NP:]), o,
                           n_seq=N_SAMPLE_SEQ, s_len=SAMPLE_LEN, t_len=t_sample, row0=NP, kv_row0=0,
                           name="attn_sample")
            x, h = _proj_res(o, w_out_ab_bf, j, x, mod, i, 2, g_post_mix[i], ffn_pre, tk=512, name="out_ab")
            caches.append((ckv[:NP].reshape(N_PROMPT_SEQ, PROMPT_LEN, KV_LORA),
                           kr[:NP, :ROPE_A].reshape(N_PROMPT_SEQ, PROMPT_LEN, ROPE_A),
                           kb[:NP].reshape(N_PROMPT_SEQ, PROMPT_LEN, KV_B, HD_B),
                           vb[:NP].reshape(N_PROMPT_SEQ, PROMPT_LEN, KV_B, HD_B)))
        else:
            y = _matmul(h, w_in_c[j], tm=1024, tn=768, out_dtype=BF16, name="in_c")
            w_z = jnp.concatenate([w_gate_fwd_a[j], w_gate_bwd_a[j], jnp.zeros((D, 128 - 2 * GATE_RANK), F32)], axis=1)
            z = _matmul(h, w_z, tm=1024, tn=128, out_dtype=F32, name="gate_lowrank")
            gate_args = (w_gate_fwd_b[j], b_gate_fwd[j], w_gate_bwd_b[j], b_gate_bwd[j])
            of, ob, sf, sb = _gla(y, z, *gate_args, None, None, None,
                                  n_seq=N_PROMPT_SEQ, s_len=PROMPT_LEN, row0=0, name="gla_prompt")
            of, ob, _, _ = _gla(y, z, *gate_args, state_gla_fwd[:, j], state_gla_bwd[:, j], (of, ob),
                                n_seq=N_SAMPLE_SEQ, s_len=SAMPLE_LEN, row0=NP, name="gla_sample")
            x, h = _gla_out(of, ob, y, g_gla_out[j], w_out_c_bf, j, x, mod, i, g_post_mix[i], ffn_pre)
            states.append((sf, sb))
        act = _ffn_up(h, w_ffn_up, ffn_conv_w, ffn_conv_b, i)
        if i + 1 < depth:
            nxt = (g_pre_mix[i + 1], i + 1, 0, 1)
            x, h = _proj_res(act, w_down_bf, i, x, mod, i, 5, g_post_ffn[i], nxt, tk=1408, name="ffn_down")
        else:
            y_prompt, _ = _proj_res(act, w_down_bf, i, x, mod, i, 5, g_post_ffn[i], None, tk=1408,
                                    name="ffn_down_prompt", row0=0, n_rows=NP)
            y_sample, _ = _proj_res(act, w_down_bf, i, x, mod, i, 5, g_post_ffn[i], None, tk=1408,
                                    name="ffn_down_sample", row0=NP, n_rows=NS)

    new_ckv, new_krope, new_k, new_v = (jnp.stack(t, axis=1) for t in zip(*caches))
    new_sf, new_sb = (jnp.stack(t, axis=1) for t in zip(*states))
    return (y_prompt.reshape(N_PROMPT_SEQ, PROMPT_LEN, D), y_sample.reshape(N_SAMPLE_SEQ, SAMPLE_LEN, D),
            new_ckv, new_krope, new_k, new_v, new_sf, new_sb)
```

```python
import functools
import math

import numpy as np
import jax
import jax.numpy as jnp
from jax import lax
from jax.experimental import pallas as pl
from jax.experimental.pallas import tpu as pltpu

D = 2048
N_PROMPT_SEQ, PROMPT_LEN = 16, 256
N_SAMPLE_SEQ, SAMPLE_LEN = 4, 2048
PAST = 512
GRID_W = 64
NP = N_PROMPT_SEQ * PROMPT_LEN
NS = N_SAMPLE_SEQ * SAMPLE_LEN
M_ROWS = NP + NS
EPS = 1e-6
ROPE_THETA = 10000.0

H_A, NOPE_A, ROPE_A, V_A = 8, 128, 64, 128
Q_LORA, KV_LORA = 512, 256
H_B, KV_B, HD_B = 8, 2, 128
IN_AB_PAD = 2560
H_C, DK_C, DV_C = 4, 256, 512
HK_C, HV_C = H_C * DK_C, H_C * DV_C
GATE_RANK = 16
GATE_TAU = 16.0
D_FF = 5632
GLA_CHUNK = 128
N_MOD_ROWS = 8

VMEM_LIMIT_BYTES = 56 * 1024 * 1024
BF16 = jnp.bfloat16
F32 = jnp.float32


def _params(*sem):
    return pltpu.CompilerParams(dimension_semantics=sem, vmem_limit_bytes=VMEM_LIMIT_BYTES)


def _mod_row(i, tm):
    n_prompt_tiles = NP // tm
    per_seq = SAMPLE_LEN // tm
    return jnp.where(i < n_prompt_tiles, 0, 1 + (i - n_prompt_tiles) // per_seq)


def _mod_spec(layer, piece, tm, tile0=0):
    def index_map(i, *_):
        return ((layer * N_MOD_ROWS + _mod_row(tile0 + i, tm)) * 6 + piece, 0, 0)
    return pl.BlockSpec((None, 1, D), index_map)


def _rms(x, g):
    return x * lax.rsqrt(jnp.mean(x * x, axis=-1, keepdims=True) + EPS) * g


def _dot(a, b):
    return jnp.dot(a, b, preferred_element_type=F32)


def _dot_nt(a, b):
    return lax.dot_general(a, b, (((1,), (1,)), ((), ())), preferred_element_type=F32)


def _split2(x):
    hi = x.astype(BF16)
    lo = (x - hi.astype(F32)).astype(BF16)
    return hi, lo


def _mod_kernel(c_ref, w_ref, b_ref, o_ref):
    c = c_ref[...]
    a = c * (1.0 / (1.0 + jnp.exp(-c)))
    a_hi, a_lo = _split2(a)
    w_hi, w_lo = _split2(w_ref[...])
    r = _dot(jnp.concatenate([a_hi, a_lo], axis=0), w_hi)
    o_ref[...] = r[:N_MOD_ROWS] + r[N_MOD_ROWS:] + _dot(a_hi, w_lo) + b_ref[...]


def _modulation(cond, w_mod, b_mod):
    depth = w_mod.shape[0]
    n = w_mod.shape[2]
    tn = 1024
    return pl.pallas_call(
        _mod_kernel,
        grid=(depth, n // tn),
        in_specs=[pl.BlockSpec((N_MOD_ROWS, D), lambda l, j: (0, 0)),
                  pl.BlockSpec((None, D, tn), lambda l, j: (l, 0, j)),
                  pl.BlockSpec((None, 1, tn), lambda l, j: (l, 0, j))],
        out_specs=pl.BlockSpec((None, N_MOD_ROWS, tn), lambda l, j: (l, 0, j)),
        out_shape=jax.ShapeDtypeStruct((depth, N_MOD_ROWS, n), F32),
        compiler_params=_params("arbitrary", "arbitrary"),
        name="modulation",
    )(cond, w_mod, b_mod.reshape(depth, 1, n))


def _prenorm_kernel(xp_ref, xs_ref, g_ref, sh_ref, sc_ref, x_ref, h_ref, *, n_prompt_tiles):
    def emit(src_ref):
        x = src_ref[...]
        x_ref[...] = x
        h_ref[...] = (_rms(x, g_ref[...]) * (1.0 + sc_ref[...]) + sh_ref[...]).astype(h_ref.dtype)

    i = pl.program_id(0)
    pl.when(i < n_prompt_tiles)(lambda: emit(xp_ref))
    pl.when(i >= n_prompt_tiles)(lambda: emit(xs_ref))


def _prenorm(x_prompt, x_sample, g, mod, layer):
    tm = 512
    npt = NP // tm
    row = pl.BlockSpec((tm, D), lambda i: (i, 0))
    return pl.pallas_call(
        functools.partial(_prenorm_kernel, n_prompt_tiles=npt),
        grid=(M_ROWS // tm,),
        in_specs=[pl.BlockSpec((tm, D), lambda i: (jnp.minimum(i, npt - 1), 0)),
                  pl.BlockSpec((tm, D), lambda i: (jnp.maximum(i - npt, 0), 0)),
                  pl.BlockSpec((1, D), lambda i: (0, 0)),
                  _mod_spec(layer, 0, tm), _mod_spec(layer, 1, tm)],
        out_specs=[row, row],
        out_shape=[jax.ShapeDtypeStruct((M_ROWS, D), F32), jax.ShapeDtypeStruct((M_ROWS, D), BF16)],
        compiler_params=_params("arbitrary"),
        name="prenorm",
    )(x_prompt.reshape(NP, D), x_sample.reshape(NS, D), g.reshape(1, D), mod, mod)


def _matmul_kernel(x_ref, w_ref, o_ref):
    o_ref[...] = _dot(x_ref[...].astype(BF16), w_ref[...].astype(BF16)).astype(o_ref.dtype)


def _matmul(x, w, *, tm, tn, out_dtype, name):
    m, k = x.shape
    n = w.shape[1]
    return pl.pallas_call(
        _matmul_kernel,
        grid=(n // tn, m // tm),
        in_specs=[pl.BlockSpec((tm, k), lambda j, i: (i, 0)),
                  pl.BlockSpec((k, tn), lambda j, i: (0, j))],
        out_specs=pl.BlockSpec((tm, tn), lambda j, i: (i, j)),
        out_shape=jax.ShapeDtypeStruct((m, n), out_dtype),
        compiler_params=_params("arbitrary", "arbitrary"),
        name=name,
    )(x, w)


def _rope_tables(rot_dim, tm):
    q = rot_dim // 4
    inv = ROPE_THETA ** (-np.arange(q, dtype=np.float64) / q)
    tok = np.arange(SAMPLE_LEN)
    ang_r = (tok // GRID_W)[:, None] * inv
    ang_c = (tok % GRID_W)[:, None] * inv
    ang = np.concatenate([ang_r, ang_r, ang_c, ang_c], axis=-1)
    cos, sin = np.cos(ang), np.sin(ang)
    first = (np.arange(rot_dim) % (2 * q)) < q
    sin_m = np.where(first, -sin, 0.0)
    sin_p = np.where(first, 0.0, sin)

    def widen(t, ident):
        full = np.full((tm + SAMPLE_LEN, 128), ident, np.float64)
        full[tm:, :rot_dim] = t
        return jnp.asarray(full, F32)

    return widen(cos, 1.0), widen(sin_m, 0.0), widen(sin_p, 0.0), q


def _rope128(x, cos, sin_m, sin_p, q):
    return (x * cos + pltpu.roll(x, 128 - q, 1) * sin_m + pltpu.roll(x, q, 1) * sin_p)


def _ab_post_kernel(y_ref, gq_ref, gkv_ref, gqn_ref, gkn_ref, wuq_ref,
                    ca_ref, ma_ref, pa_ref, cb_ref, mb_ref, pb_ref,
                    qa_ref, qb_ref, ckv_ref, kr_ref, kb_ref, vb_ref, *, qa_rot, qb_rot):
    ca, ma, pa = ca_ref[...], ma_ref[...], pa_ref[...]
    cb, mb, pb = cb_ref[...], mb_ref[...], pb_ref[...]
    scale_a = (NOPE_A + ROPE_A) ** -0.5 * math.log2(math.e)
    scale_b = HD_B ** -0.5 * math.log2(math.e)

    cq = _rms(y_ref[:, 0:Q_LORA], gq_ref[...])
    qa = _dot(cq.astype(BF16), wuq_ref[...].astype(BF16)) * scale_a
    for h in range(H_A):
        base = h * 256
        qa_ref[:, base:base + 128] = qa[:, base:base + 128].astype(qa_ref.dtype)
        qa_ref[:, base + 128:base + 256] = _rope128(
            qa[:, base + 128:base + 256], ca, ma, pa, qa_rot).astype(qa_ref.dtype)

    ckv_ref[...] = _rms(y_ref[:, 512:768], gkv_ref[...])
    kr_ref[...] = _rope128(y_ref[:, 2304:2432], ca, ma, pa, qa_rot)

    for h in range(H_B):
        xb = _rms(y_ref[:, 768 + h * 128:768 + (h + 1) * 128], gqn_ref[...])
        qb_ref[:, h * 128:(h + 1) * 128] = (
            _rope128(xb, cb, mb, pb, qb_rot) * scale_b).astype(qb_ref.dtype)
    for g in range(KV_B):
        xk = _rms(y_ref[:, 1792 + g * 128:1792 + (g + 1) * 128], gkn_ref[...])
        kb_ref[:, g * 128:(g + 1) * 128] = _rope128(xk, cb, mb, pb, qb_rot)
    vb_ref[...] = y_ref[:, 2048:2304]


def _ab_post(y, g_q, g_kv, g_qn, g_kn, w_uq_pad):
    tm = 256
    ca, ma, pa, qa_rot = _rope_tables(ROPE_A, tm)
    cb, mb, pb, qb_rot = _rope_tables(HD_B, tm)
    n_prompt_tiles = NP // tm
    per_seq = SAMPLE_LEN // tm

    def tab_map(i):
        return (jnp.where(i < n_prompt_tiles, 0, 1 + (i - n_prompt_tiles) % per_seq), 0)

    tab = pl.BlockSpec((tm, 128), tab_map)
    row = lambda w: pl.BlockSpec((tm, w), lambda i: (i, 0))
    vec = lambda w: pl.BlockSpec((1, w), lambda i: (0, 0))
    return pl.pallas_call(
        functools.partial(_ab_post_kernel, qa_rot=qa_rot, qb_rot=qb_rot),
        grid=(M_ROWS // tm,),
        in_specs=[row(IN_AB_PAD), vec(Q_LORA), vec(KV_LORA), vec(HD_B), vec(HD_B),
                  pl.BlockSpec((Q_LORA, H_A * 256), lambda i: (0, 0)),
                  tab, tab, tab, tab, tab, tab],
        out_specs=[row(H_A * 256), row(H_B * HD_B), row(KV_LORA), row(128),
                   row(KV_B * HD_B), row(KV_B * HD_B)],
        out_shape=[jax.ShapeDtypeStruct((M_ROWS, H_A * 256), BF16),
                   jax.ShapeDtypeStruct((M_ROWS, H_B * HD_B), BF16),
                   jax.ShapeDtypeStruct((M_ROWS, KV_LORA), F32),
                   jax.ShapeDtypeStruct((M_ROWS, 128), F32),
                   jax.ShapeDtypeStruct((M_ROWS, KV_B * HD_B), F32),
                   jax.ShapeDtypeStruct((M_ROWS, KV_B * HD_B), F32)],
        compiler_params=_params("arbitrary"),
        name="ab_post",
    )(y, g_q.reshape(1, -1), g_kv.reshape(1, -1), g_qn.reshape(1, -1), g_kn.reshape(1, -1),
      w_uq_pad, ca, ma, pa, cb, mb, pb)


def _softmax_pv(s, v):
    m = jnp.max(s, axis=-1, keepdims=True)
    e = jnp.exp2(s - m)
    l = jnp.sum(e, axis=-1, keepdims=True)
    return _dot(e.astype(BF16), v) / l


def _attn_kernel(qa_ref, qb_ref, kv_ref, kr_ref, kb_ref, vb_ref, *rest):
    o_ref = rest[-1]
    kr = kr_ref[...]
    rep = H_B // KV_B
    n_heads = H_A + H_B

    def scores(h):
        if h < H_A:
            k = jnp.concatenate([kv_ref[:, h * 256:h * 256 + 128], kr], axis=1)
            return _dot_nt(qa_ref[:, h * 256:(h + 1) * 256], k)
        g = (h - H_A) // rep
        return _dot_nt(qb_ref[:, (h - H_A) * 128:(h - H_A + 1) * 128], kb_ref[:, g * 128:(g + 1) * 128])

    def values(h):
        if h < H_A:
            return kv_ref[:, h * 256 + 128:(h + 1) * 256]
        g = (h - H_A) // rep
        return vb_ref[:, g * 128:(g + 1) * 128]

    s_next = scores(0)
    for h in range(n_heads):
        s_cur = s_next
        if h + 1 < n_heads:
            s_next = scores(h + 1)
        o_ref[:, h * 128:(h + 1) * 128] = _softmax_pv(s_cur, values(h)).astype(o_ref.dtype)


def _attention(qa, qb, kv, kr, kb, vb, o_prev, *, n_seq, s_len, t_len, row0, kv_row0, name):
    tq = 256
    nq = s_len // tq
    q0 = row0 // tq
    kv0 = kv_row0 // t_len
    qspec = lambda w: pl.BlockSpec((tq, w), lambda b, i: (q0 + b * nq + i, 0))
    kspec = lambda w, b0=0: pl.BlockSpec((t_len, w), lambda b, i: (b0 + b, 0), pipeline_mode=pl.Buffered(1))
    in_specs = [qspec(H_A * 256), qspec(H_B * HD_B), kspec(H_A * 256, kv0), kspec(128),
                kspec(KV_B * HD_B), kspec(KV_B * HD_B)]
    args = [qa, qb, kv, kr, kb, vb]
    aliases = {}
    if o_prev is not None:
        in_specs.append(pl.BlockSpec(memory_space=pl.ANY))
        args.append(o_prev)
        aliases = {len(args) - 1: 0}
    return pl.pallas_call(
        _attn_kernel,
        grid=(n_seq, nq),
        in_specs=in_specs,
        out_specs=pl.BlockSpec((tq, D), lambda b, i: (q0 + b * nq + i, 0)),
        out_shape=jax.ShapeDtypeStruct((M_ROWS, D), BF16),
        input_output_aliases=aliases,
        compiler_params=_params("arbitrary", "arbitrary"),
        name=name,
    )(*args)


def _residual_epilogue(acc, x_ref, gpost_ref, gate_ref, nxt, xo_ref, ho_ref):
    x_new = x_ref[...] + gate_ref[...] * _rms(acc, gpost_ref[...])
    xo_ref[...] = x_new
    if nxt is not None:
        gn_ref, sh_ref, sc_ref = nxt
        ho_ref[...] = (_rms(x_new, gn_ref[...]) * (1.0 + sc_ref[...]) + sh_ref[...]).astype(ho_ref.dtype)


def _proj_res_kernel(*refs, emit_h):
    if emit_h:
        a_ref, w_ref, x_ref, gpost_ref, gate_ref, gn_ref, sh_ref, sc_ref, xo_ref, ho_ref = refs
        nxt = (gn_ref, sh_ref, sc_ref)
    else:
        a_ref, w_ref, x_ref, gpost_ref, gate_ref, xo_ref = refs
        nxt, ho_ref = None, None
    _residual_epilogue(_dot(a_ref[...], w_ref[...]), x_ref, gpost_ref, gate_ref, nxt, xo_ref, ho_ref)


def _res_specs(tm, tile0, n_rows, mod, layer, gate_piece, g_post, nxt):
    vec = pl.BlockSpec((1, D), lambda i, *_: (0, 0))
    specs = [pl.BlockSpec((tm, D), lambda i, *_: (tile0 + i, 0)), vec, _mod_spec(layer, gate_piece, tm, tile0)]
    args = [g_post.reshape(1, D), mod]
    if nxt is not None:
        g_next, layer_next, sh_piece, sc_piece = nxt
        specs += [vec, _mod_spec(layer_next, sh_piece, tm, tile0), _mod_spec(layer_next, sc_piece, tm, tile0)]
        args += [g_next.reshape(1, D), mod, mod]
    out_specs = [pl.BlockSpec((tm, D), lambda i, *_: (i, 0))]
    out_shape = [jax.ShapeDtypeStruct((n_rows, D), F32)]
    if nxt is not None:
        out_specs.append(pl.BlockSpec((tm, D), lambda i, *_: (i, 0)))
        out_shape.append(jax.ShapeDtypeStruct((n_rows, D), BF16))
    return specs, args, out_specs, out_shape


def _resident_weight_spec(kdim, w_layer):
    return pl.BlockSpec((None, kdim, D), lambda i: (w_layer, 0, 0), pipeline_mode=pl.Buffered(1))


def _proj_res(a, w, w_layer, x, mod, layer, gate_piece, g_post, nxt, *, tm, name, row0=0, n_rows=M_ROWS):
    tile0 = row0 // tm
    kdim = a.shape[1]
    specs, args, out_specs, out_shape = _res_specs(tm, tile0, n_rows, mod, layer, gate_piece, g_post, nxt)
    out = pl.pallas_call(
        functools.partial(_proj_res_kernel, emit_h=nxt is not None),
        grid=(n_rows // tm,),
        in_specs=[pl.BlockSpec((tm, kdim), lambda i: (tile0 + i, 0)),
                  _resident_weight_spec(kdim, w_layer)] + specs,
        out_specs=out_specs,
        out_shape=out_shape,
        compiler_params=_params("arbitrary"),
        name=name,
    )(a, w, x, *args)
    return out if nxt is not None else (out[0], None)


def _ffn_up_kernel(h_ref, hp_ref, hn_ref, wa_ref, wg_ref, cwa_ref, cwg_ref, cba_ref, cbg_ref,
                   mp_ref, mn_ref, o_ref, wbf_ref):
    tm = h_ref.shape[0]
    halo = hp_ref.shape[0]

    @pl.when(pl.program_id(1) == 0)
    def _():
        wbf_ref[0] = wa_ref[...].astype(BF16)
        wbf_ref[1] = wg_ref[...].astype(BF16)

    h = h_ref[...]
    edge = jnp.concatenate([hp_ref[...], hn_ref[...]], axis=0)
    mp = mp_ref[...]
    mn = mn_ref[...]
    row = lax.broadcasted_iota(jnp.int32, (tm, 1), 0)

    def branch(which, cw_ref, cb_ref):
        w = wbf_ref[which]
        u = _dot(h, w)
        u_edge = _dot(edge, w)
        prev = jnp.where(row == 0, u_edge[halo - 1:halo], pltpu.roll(u, 1, 0)) * mp
        nxt = jnp.where(row == tm - 1, u_edge[halo:halo + 1], pltpu.roll(u, tm - 1, 0)) * mn
        cw = cw_ref[...]
        return prev * cw[0:1] + u * cw[1:2] + nxt * cw[2:3] + cb_ref[...]

    a = branch(0, cwa_ref, cba_ref)
    g = branch(1, cwg_ref, cbg_ref)
    o_ref[...] = (g * (1.0 / (1.0 + jnp.exp(-g))) * a).astype(o_ref.dtype)


def _seq_edge_masks():
    pos = np.concatenate([np.tile(np.arange(PROMPT_LEN), N_PROMPT_SEQ),
                          np.tile(np.arange(SAMPLE_LEN), N_SAMPLE_SEQ)])
    lens = np.concatenate([np.full(NP, PROMPT_LEN), np.full(NS, SAMPLE_LEN)])
    has_prev = (pos > 0).astype(np.float32).reshape(M_ROWS, 1)
    has_next = (pos < lens - 1).astype(np.float32).reshape(M_ROWS, 1)
    return jnp.asarray(has_prev), jnp.asarray(has_next)


def _ffn_up(h, w_up, conv_w, conv_b, layer):
    tm, tn, halo = 1024, 512, 16
    nj = D_FF // tn
    r = tm // halo
    last = M_ROWS // halo - 1
    has_prev, has_next = _seq_edge_masks()
    conv_b = conv_b.reshape(conv_b.shape[0], 1, 2 * D_FF)
    col = lambda rows, half: pl.BlockSpec((None, rows, tn), lambda j, i: (layer, 0, j + half * nj))
    return pl.pallas_call(
        _ffn_up_kernel,
        grid=(nj, M_ROWS // tm),
        in_specs=[pl.BlockSpec((tm, D), lambda j, i: (i, 0)),
                  pl.BlockSpec((halo, D), lambda j, i: (jnp.maximum(i * r - 1, 0), 0)),
                  pl.BlockSpec((halo, D), lambda j, i: (jnp.minimum((i + 1) * r, last), 0)),
                  col(D, 0), col(D, 1), col(3, 0), col(3, 1), col(1, 0), col(1, 1),
                  pl.BlockSpec((tm, 1), lambda j, i: (i, 0)),
                  pl.BlockSpec((tm, 1), lambda j, i: (i, 0))],
        out_specs=pl.BlockSpec((tm, tn), lambda j, i: (i, j)),
        out_shape=jax.ShapeDtypeStruct((M_ROWS, D_FF), BF16),
        scratch_shapes=[pltpu.VMEM((2, D, tn), BF16)],
        compiler_params=_params("arbitrary", "arbitrary"),
        name="ffn_up",
    )(h, h, h, w_up, w_up, conv_w, conv_w, conv_b, conv_b, has_prev, has_next)


def _gla_masks(c):
    levels = int(math.log2(c))
    t = np.arange(c)
    pair = [t[:, None] == t[None, :]]
    later = []
    for l in range(1, levels + 1):
        blk = t >> l
        up = ((t >> (l - 1)) & 1) == 1
        pair.append((blk[:, None] == blk[None, :]) & up[:, None] & ~up[None, :])
        later.append(np.broadcast_to(up[:, None], (c, DK_C)))
    pair = np.stack(pair).astype(np.float32)
    pair = np.stack([pair, pair.transpose(0, 2, 1)])
    return jnp.asarray(pair), jnp.asarray(np.stack(later).astype(np.float32))


def _gla_chain(q, k, v, za, wb, ba, pair_ref, later_ref, s_ref, reverse):
    c = q.shape[0]
    levels = later_ref.shape[0]
    za_hi, za_lo = _split2(za)
    wb_hi, wb_lo = _split2(wb)
    z = _dot(za_hi, wb_hi) + _dot(za_lo, wb_hi) + _dot(za_hi, wb_lo) + ba
    x = (jnp.minimum(z, 0.0) - jnp.log(1.0 + jnp.exp(-jnp.abs(z)))) * (math.log2(math.e) / GATE_TAU)
    yield None

    rev = 1 if reverse else 0
    first_i, second_i = (1, 0) if reverse else (0, 1)
    p, t = x, x
    k_bf = k.astype(BF16)
    a = pair_ref[rev, 0] * _dot_nt(q.astype(BF16), k_bf)
    for l in range(1, levels + 1):
        s = 1 << (l - 1)
        if s < 8:
            later = later_ref[l - 1] > 0.5
            t_sib = jnp.where(later, pltpu.roll(t, s, 0), pltpu.roll(t, c - s, 0))
            if reverse:
                d = jnp.where(later, t - p, p)
                p = jnp.where(later, p, p + t_sib)
            else:
                d = jnp.where(later, p, t - p)
                p = jnp.where(later, p + t_sib, p)
            t = t + t_sib
            e = jnp.exp2(d)
            a = a + pair_ref[rev, l] * _dot_nt((q * e).astype(BF16), (k * e).astype(BF16))
        else:
            nb = c // (2 * s)
            split = lambda arr: arr.reshape(nb, 2, s, arr.shape[-1])
            join = lambda f, sec: jnp.stack([f, sec] if first_i == 0 else [sec, f], axis=1)
            p4, t4 = split(p), split(t)
            p_f, p_s, t_f, t_s = p4[:, first_i], p4[:, second_i], t4[:, first_i], t4[:, second_i]
            q_s = split(q)[:, second_i] * jnp.exp2(p_s)
            k_f = split(k)[:, first_i] * jnp.exp2(t_f - p_f)
            k_all = join(k_f.astype(BF16), split(k_bf)[:, second_i]).reshape(c, DK_C)
            r = _dot_nt(q_s.reshape(c // 2, DK_C).astype(BF16), k_all).reshape(nb, s, c)
            a4 = a.reshape(nb, 2, s, c)
            mask = pair_ref[rev, l].reshape(nb, 2, s, c)[:, second_i]
            a = join(a4[:, first_i], a4[:, second_i] + mask * r).reshape(c, c)
            t_new = t_f + t_s
            p = join(p_f, p_s + t_f).reshape(c, DK_C)
            t = join(t_new, t_new).reshape(c, DK_C)
        yield None
    s_old = s_ref[...]
    o_inter = _dot((q * jnp.exp2(p)).astype(BF16), s_old.astype(BF16))
    kd_t = (k * jnp.exp2(t - p)).T.astype(BF16)
    both = _dot(jnp.concatenate([a.astype(BF16), kd_t], axis=0), v)
    dec_col = jnp.exp2(t[0:8]).T[:, 0:1]
    s_ref[...] = s_old * dec_col + both[c:]
    yield both[:c] + o_inter


def _gla_kernel(*refs, zero_init, aliased):
    (qf_ref, kf_ref, vf_ref, zf_ref, qb_ref, kb_ref, vb_ref, zb_ref,
     wbf_ref, baf_ref, wbb_ref, bab_ref, pair_ref, later_ref) = refs[:14]
    rest = refs[14:]
    if not zero_init:
        sfi_ref, sbi_ref = rest[:2]
        rest = rest[2:]
    if aliased:
        rest = rest[2:]
    of_ref, ob_ref, sfo_ref, sbo_ref, sf_ref, sb_ref = rest
    ci = pl.program_id(1)

    @pl.when(ci == 0)
    def _():
        if zero_init:
            sf_ref[...] = jnp.zeros_like(sf_ref)
            sb_ref[...] = jnp.zeros_like(sb_ref)
        else:
            sf_ref[...] = sfi_ref[...]
            sb_ref[...] = sbi_ref[...]

    scale = DK_C ** -0.5
    chains = []
    for h in range(H_C):
        ks, vs = slice(h * DK_C, (h + 1) * DK_C), slice(h * DV_C, (h + 1) * DV_C)
        chains.append((of_ref, vs, _gla_chain(
            qf_ref[:, ks].astype(F32) * scale, kf_ref[:, ks].astype(F32), vf_ref[:, vs],
            zf_ref[:, 0:GATE_RANK], wbf_ref[:, ks], baf_ref[:, ks], pair_ref, later_ref, sf_ref.at[h], False)))
        chains.append((ob_ref, vs, _gla_chain(
            qb_ref[:, ks].astype(F32) * scale, kb_ref[:, ks].astype(F32), vb_ref[:, vs],
            zb_ref[:, GATE_RANK:2 * GATE_RANK], wbb_ref[:, ks], bab_ref[:, ks], pair_ref, later_ref,
            sb_ref.at[h], True)))
    live = True
    while live:
        for o_ref, vs, chain in chains:
            out = next(chain, "done")
            if isinstance(out, str):
                live = False
            elif out is not None:
                o_ref[:, vs] = out.astype(o_ref.dtype)

    @pl.when(ci == pl.num_programs(1) - 1)
    def _():
        sfo_ref[...] = sf_ref[...]
        sbo_ref[...] = sb_ref[...]


def _gla(y, z, wb_f, ba_f, wb_b, ba_b, s_f, s_b, o_prev, *, n_seq, s_len, row0, name):
    c = GLA_CHUNK
    n = s_len // c
    r0 = row0 // c
    zero_init = s_f is None
    pair, later = _gla_masks(c)
    fwd = lambda b, i: r0 + b * n + i
    bwd = lambda b, i: r0 + b * n + (n - 1 - i)

    def dir_specs(rows):
        return [pl.BlockSpec((c, HK_C), lambda b, i: (rows(b, i), 0)),
                pl.BlockSpec((c, HK_C), lambda b, i: (rows(b, i), 1)),
                pl.BlockSpec((c, HV_C), lambda b, i: (rows(b, i), 1)),
                pl.BlockSpec((c, 128), lambda b, i: (rows(b, i), 0))]

    full = lambda arr: pl.BlockSpec(arr.shape, lambda b, i: (0,) * arr.ndim)
    state = pl.BlockSpec((None, H_C, DK_C, DV_C), lambda b, i: (b, 0, 0, 0))
    ba_f, ba_b = ba_f.reshape(1, -1), ba_b.reshape(1, -1)
    in_specs = (dir_specs(fwd) + dir_specs(bwd)
                + [full(wb_f), full(ba_f), full(wb_b), full(ba_b), full(pair), full(later)])
    args = [y, y, y, z, y, y, y, z, wb_f, ba_f, wb_b, ba_b, pair, later]
    if not zero_init:
        in_specs += [state, state]
        args += [s_f, s_b]
    aliases = {}
    if o_prev is not None:
        in_specs += [pl.BlockSpec(memory_space=pl.ANY)] * 2
        aliases = {len(args): 0, len(args) + 1: 1}
        args += list(o_prev)
    return pl.pallas_call(
        functools.partial(_gla_kernel, zero_init=zero_init, aliased=o_prev is not None),
        grid=(n_seq, n),
        in_specs=in_specs,
        out_specs=[pl.BlockSpec((c, HV_C), lambda b, i: (fwd(b, i), 0)),
                   pl.BlockSpec((c, HV_C), lambda b, i: (bwd(b, i), 0)),
                   state, state],
        out_shape=[jax.ShapeDtypeStruct((M_ROWS, HV_C), BF16),
                   jax.ShapeDtypeStruct((M_ROWS, HV_C), BF16),
                   jax.ShapeDtypeStruct((n_seq, H_C, DK_C, DV_C), F32),
                   jax.ShapeDtypeStruct((n_seq, H_C, DK_C, DV_C), F32)],
        scratch_shapes=[pltpu.VMEM((H_C, DK_C, DV_C), F32), pltpu.VMEM((H_C, DK_C, DV_C), F32)],
        input_output_aliases=aliases,
        compiler_params=_params("arbitrary", "arbitrary"),
        name=name,
    )(*args)


def _gla_out_kernel(of_ref, ob_ref, r_ref, go_ref, w_ref, x_ref, gpost_ref, gate_ref,
                    gn_ref, sh_ref, sc_ref, xo_ref, ho_ref, a_ref):
    for h in range(H_C):
        vs = slice(h * DV_C, (h + 1) * DV_C)
        r = r_ref[:, vs].astype(F32)
        o = of_ref[:, vs].astype(F32) + ob_ref[:, vs].astype(F32)
        a = _rms(o, go_ref[...]) * (r * (1.0 / (1.0 + jnp.exp(-r))))
        a_ref[:, vs] = a.astype(BF16)
    _residual_epilogue(_dot(a_ref[...], w_ref[...]), x_ref, gpost_ref, gate_ref,
                       (gn_ref, sh_ref, sc_ref), xo_ref, ho_ref)


def _gla_out(o_f, o_b, y, g_out, w, w_layer, x, mod, layer, g_post, nxt):
    tm = 256
    r_blk = (2 * HK_C + HV_C) // HV_C
    specs, args, out_specs, out_shape = _res_specs(tm, 0, M_ROWS, mod, layer, 2, g_post, nxt)
    row = pl.BlockSpec((tm, HV_C), lambda i: (i, 0))
    return pl.pallas_call(
        _gla_out_kernel,
        grid=(M_ROWS // tm,),
        in_specs=[row, row,
                  pl.BlockSpec((tm, HV_C), lambda i: (i, r_blk)),
                  pl.BlockSpec((1, DV_C), lambda i: (0, 0)),
                  _resident_weight_spec(HV_C, w_layer)] + specs,
        out_specs=out_specs,
        out_shape=out_shape,
        scratch_shapes=[pltpu.VMEM((tm, HV_C), BF16)],
        compiler_params=_params("arbitrary"),
        name="gla_out",
    )(o_f, o_b, y, g_out.reshape(1, -1), w, x, *args)


def _with_cache(cache, new):
    w = new.shape[-1]
    cat = jnp.concatenate([cache.reshape(N_SAMPLE_SEQ, PAST, w).astype(BF16),
                           new.reshape(N_SAMPLE_SEQ, SAMPLE_LEN, w).astype(BF16)], axis=1)
    return cat.reshape(N_SAMPLE_SEQ * (PAST + SAMPLE_LEN), w)


def kernel(x_prompt, x_sample, cache_mla_ckv, cache_mla_krope, cache_gqa_k, cache_gqa_v, state_gla_fwd, state_gla_bwd, c, c_ctx, w_mod, b_mod, g_pre_mix, g_post_mix, g_pre_ffn, g_post_ffn, w_in_ab, g_mla_q, g_mla_kv, w_mla_uq, w_mla_ukv, g_gqa_q, g_gqa_k, w_out_ab, w_in_c, w_gate_fwd_a, w_gate_fwd_b, b_gate_fwd, w_gate_bwd_a, w_gate_bwd_b, b_gate_bwd, g_gla_out, w_out_c, w_ffn_up, ffn_conv_w, ffn_conv_b, w_ffn_down):
    depth = w_mod.shape[0]
    cond = jnp.concatenate([c_ctx[None], c, jnp.zeros((N_MOD_ROWS - 1 - N_SAMPLE_SEQ, D), F32)], axis=0)
    mod = _modulation(cond, w_mod, b_mod).reshape(depth * N_MOD_ROWS * 6, 1, D)
    w_out_ab_bf, w_out_c_bf, w_down_bf = w_out_ab.astype(BF16), w_out_c.astype(BF16), w_ffn_down.astype(BF16)
    t_sample = PAST + SAMPLE_LEN

    x, h = _prenorm(x_prompt, x_sample, g_pre_mix[0], mod, 0)
    caches, states = [], []
    for i in range(depth):
        j = i // 2
        ffn_pre = (g_pre_ffn[i], i, 3, 4)
        if i % 2 == 0:
            w = w_in_ab[j]
            w_in = jnp.concatenate(
                [w[:, 0:768], w[:, 832:2368], w[:, 768:832], jnp.zeros((D, IN_AB_PAD - 2368), F32)], axis=1)
            wq = w_mla_uq[j].reshape(Q_LORA, H_A, NOPE_A + ROPE_A)
            w_uq = jnp.concatenate([wq, jnp.zeros((Q_LORA, H_A, 256 - NOPE_A - ROPE_A), F32)],
                                   axis=2).reshape(Q_LORA, H_A * 256)
            y = _matmul(h, w_in, tm=1024, tn=512, out_dtype=F32, name="in_ab")
            qa, qb, ckv, kr, kb, vb = _ab_post(y, g_mla_q[j], g_mla_kv[j], g_gqa_q[j], g_gqa_k[j], w_uq)
            ckv_all = jnp.concatenate([_with_cache(cache_mla_ckv[:, j], ckv[NP:]), ckv[:NP].astype(BF16)], axis=0)
            kv = _matmul(ckv_all, w_mla_ukv[j], tm=1024, tn=1024, out_dtype=BF16, name="kv_up")
            o = _attention(qa, qb, kv, kr[:NP].astype(BF16), kb[:NP].astype(BF16), vb[:NP].astype(BF16), None,
                           n_seq=N_PROMPT_SEQ, s_len=PROMPT_LEN, t_len=PROMPT_LEN, row0=0,
                           kv_row0=N_SAMPLE_SEQ * t_sample, name="attn_prompt")
            kr_cache = jnp.pad(cache_mla_krope[:, j], ((0, 0), (0, 0), (0, 128 - ROPE_A)))
            o = _attention(qa, qb, kv, _with_cache(kr_cache, kr[NP:]),
                           _with_cache(cache_gqa_k[:, j], kb[NP:]), _with_cache(cache_gqa_v[:, j], vb[NP:]), o,
                           n_seq=N_SAMPLE_SEQ, s_len=SAMPLE_LEN, t_len=t_sample, row0=NP, kv_row0=0,
                           name="attn_sample")
            x, h = _proj_res(o, w_out_ab_bf, j, x, mod, i, 2, g_post_mix[i], ffn_pre, tm=512, name="out_ab")
            caches.append((ckv[:NP].reshape(N_PROMPT_SEQ, PROMPT_LEN, KV_LORA),
                           kr[:NP, :ROPE_A].reshape(N_PROMPT_SEQ, PROMPT_LEN, ROPE_A),
                           kb[:NP].reshape(N_PROMPT_SEQ, PROMPT_LEN, KV_B, HD_B),
                           vb[:NP].reshape(N_PROMPT_SEQ, PROMPT_LEN, KV_B, HD_B)))
        else:
            y = _matmul(h, w_in_c[j], tm=1024, tn=768, out_dtype=BF16, name="in_c")
            w_z = jnp.concatenate([w_gate_fwd_a[j], w_gate_bwd_a[j], jnp.zeros((D, 128 - 2 * GATE_RANK), F32)], axis=1)
            z = _matmul(h, w_z, tm=1024, tn=128, out_dtype=F32, name="gate_lowrank")
            gate_args = (w_gate_fwd_b[j], b_gate_fwd[j], w_gate_bwd_b[j], b_gate_bwd[j])
            of, ob, sf, sb = _gla(y, z, *gate_args, None, None, None,
                                  n_seq=N_PROMPT_SEQ, s_len=PROMPT_LEN, row0=0, name="gla_prompt")
            of, ob, _, _ = _gla(y, z, *gate_args, state_gla_fwd[:, j], state_gla_bwd[:, j], (of, ob),
                                n_seq=N_SAMPLE_SEQ, s_len=SAMPLE_LEN, row0=NP, name="gla_sample")
            x, h = _gla_out(of, ob, y, g_gla_out[j], w_out_c_bf, j, x, mod, i, g_post_mix[i], ffn_pre)
            states.append((sf, sb))
        act = _ffn_up(h, w_ffn_up, ffn_conv_w, ffn_conv_b, i)
        if i + 1 < depth:
            nxt = (g_pre_mix[i + 1], i + 1, 0, 1)
            x, h = _proj_res(act, w_down_bf, i, x, mod, i, 5, g_post_ffn[i], nxt, tm=256, name="ffn_down")
        else:
            y_prompt, _ = _proj_res(act, w_down_bf, i, x, mod, i, 5, g_post_ffn[i], None, tm=256,
                                    name="ffn_down_prompt", row0=0, n_rows=NP)
            y_sample, _ = _proj_res(act, w_down_bf, i, x, mod, i, 5, g_post_ffn[i], None, tm=256,
                                    name="ffn_down_sample", row0=NP, n_rows=NS)

    new_ckv, new_krope, new_k, new_v = (jnp.stack(t, axis=1) for t in zip(*caches))
    new_sf, new_sb = (jnp.stack(t, axis=1) for t in zip(*states))
    return (y_prompt.reshape(N_PROMPT_SEQ, PROMPT_LEN, D), y_sample.reshape(N_SAMPLE_SEQ, SAMPLE_LEN, D),
            new_ckv, new_krope, new_k, new_v, new_sf, new_sb)
```

```python
import functools
import math

import numpy as np
import jax
import jax.numpy as jnp
from jax import lax
from jax.experimental import pallas as pl
from jax.experimental.pallas import tpu as pltpu

D = 2048
N_PROMPT_SEQ, PROMPT_LEN = 16, 256
N_SAMPLE_SEQ, SAMPLE_LEN = 4, 2048
PAST = 512
GRID_W = 64
NP = N_PROMPT_SEQ * PROMPT_LEN
NS = N_SAMPLE_SEQ * SAMPLE_LEN
M_ROWS = NP + NS
EPS = 1e-6
ROPE_THETA = 10000.0

H_A, NOPE_A, ROPE_A, V_A = 8, 128, 64, 128
Q_LORA, KV_LORA = 512, 256
H_B, KV_B, HD_B = 8, 2, 128
IN_AB_PAD = 2560
H_C, DK_C, DV_C = 4, 256, 512
HK_C, HV_C = H_C * DK_C, H_C * DV_C
GATE_RANK = 16
GATE_TAU = 16.0
D_FF = 5632
GLA_CHUNK = 128
GLA_FINE = 3
N_MOD_ROWS = 8

VMEM_LIMIT_BYTES = 56 * 1024 * 1024
BF16 = jnp.bfloat16
F32 = jnp.float32


def _params(*sem):
    return pltpu.CompilerParams(dimension_semantics=sem, vmem_limit_bytes=VMEM_LIMIT_BYTES)


def _mod_row(i, tm):
    n_prompt_tiles = NP // tm
    per_seq = SAMPLE_LEN // tm
    return jnp.where(i < n_prompt_tiles, 0, 1 + (i - n_prompt_tiles) // per_seq)


def _mod_spec(layer, piece, tm, tile0=0):
    def index_map(i, *_):
        return ((layer * N_MOD_ROWS + _mod_row(tile0 + i, tm)) * 6 + piece, 0, 0)
    return pl.BlockSpec((None, 1, D), index_map)


def _rms(x, g):
    return x * lax.rsqrt(jnp.mean(x * x, axis=-1, keepdims=True) + EPS) * g


def _dot(a, b):
    return jnp.dot(a, b, preferred_element_type=F32)


def _dot_nt(a, b):
    return lax.dot_general(a, b, (((1,), (1,)), ((), ())), preferred_element_type=F32)


def _split2(x):
    hi = x.astype(BF16)
    lo = (x - hi.astype(F32)).astype(BF16)
    return hi, lo


def _mod_kernel(c_ref, w_ref, b_ref, o_ref):
    c = c_ref[...]
    a = c * (1.0 / (1.0 + jnp.exp(-c)))
    a_hi, a_lo = _split2(a)
    w_hi, w_lo = _split2(w_ref[...])
    r = _dot(jnp.concatenate([a_hi, a_lo], axis=0), w_hi)
    o_ref[...] = r[:N_MOD_ROWS] + r[N_MOD_ROWS:] + _dot(a_hi, w_lo) + b_ref[...]


def _modulation(cond, w_mod, b_mod):
    depth = w_mod.shape[0]
    n = w_mod.shape[2]
    tn = 1024
    return pl.pallas_call(
        _mod_kernel,
        grid=(depth, n // tn),
        in_specs=[pl.BlockSpec((N_MOD_ROWS, D), lambda l, j: (0, 0)),
                  pl.BlockSpec((None, D, tn), lambda l, j: (l, 0, j)),
                  pl.BlockSpec((None, 1, tn), lambda l, j: (l, 0, j))],
        out_specs=pl.BlockSpec((None, N_MOD_ROWS, tn), lambda l, j: (l, 0, j)),
        out_shape=jax.ShapeDtypeStruct((depth, N_MOD_ROWS, n), F32),
        compiler_params=_params("arbitrary", "arbitrary"),
        name="modulation",
    )(cond, w_mod, b_mod.reshape(depth, 1, n))


def _prenorm_kernel(xp_ref, xs_ref, g_ref, sh_ref, sc_ref, h_ref):
    x = _row_load([xp_ref, xs_ref])
    h_ref[...] = (_rms(x, g_ref[...]) * (1.0 + sc_ref[...]) + sh_ref[...]).astype(h_ref.dtype)


def _prenorm(x_pair, g, mod, layer):
    tm = 512
    x_specs, x_args = _row_specs(x_pair, tm)
    return pl.pallas_call(
        _prenorm_kernel,
        grid=(M_ROWS // tm,),
        in_specs=x_specs + [pl.BlockSpec((1, D), lambda i: (0, 0)),
                            _mod_spec(layer, 0, tm), _mod_spec(layer, 1, tm)],
        out_specs=pl.BlockSpec((tm, D), lambda i: (i, 0)),
        out_shape=jax.ShapeDtypeStruct((M_ROWS, D), BF16),
        compiler_params=_params("arbitrary"),
        name="prenorm",
    )(*x_args, g.reshape(1, D), mod, mod)


def _matmul_kernel(x_ref, w_ref, o_ref):
    o_ref[...] = _dot(x_ref[...].astype(BF16), w_ref[...].astype(BF16)).astype(o_ref.dtype)


def _matmul(x, w, *, tm, tn, out_dtype, name):
    m, k = x.shape
    n = w.shape[1]
    return pl.pallas_call(
        _matmul_kernel,
        grid=(n // tn, m // tm),
        in_specs=[pl.BlockSpec((tm, k), lambda j, i: (i, 0)),
                  pl.BlockSpec((k, tn), lambda j, i: (0, j))],
        out_specs=pl.BlockSpec((tm, tn), lambda j, i: (i, j)),
        out_shape=jax.ShapeDtypeStruct((m, n), out_dtype),
        compiler_params=_params("arbitrary", "arbitrary"),
        name=name,
    )(x, w)


def _rope_tables(rot_dim, tm):
    q = rot_dim // 4
    inv = ROPE_THETA ** (-np.arange(q, dtype=np.float64) / q)
    tok = np.arange(SAMPLE_LEN)
    ang_r = (tok // GRID_W)[:, None] * inv
    ang_c = (tok % GRID_W)[:, None] * inv
    ang = np.concatenate([ang_r, ang_r, ang_c, ang_c], axis=-1)
    cos, sin = np.cos(ang), np.sin(ang)
    first = (np.arange(rot_dim) % (2 * q)) < q
    sin_m = np.where(first, -sin, 0.0)
    sin_p = np.where(first, 0.0, sin)

    def widen(t, ident):
        full = np.full((tm + SAMPLE_LEN, 128), ident, np.float64)
        full[tm:, :rot_dim] = t
        return jnp.asarray(full, F32)

    return widen(cos, 1.0), widen(sin_m, 0.0), widen(sin_p, 0.0), q


def _rope128(x, cos, sin_m, sin_p, q):
    return (x * cos + pltpu.roll(x, 128 - q, 1) * sin_m + pltpu.roll(x, q, 1) * sin_p)


def _ab_post_kernel(y_ref, gq_ref, gkv_ref, gqn_ref, gkn_ref, wuq_ref,
                    ca_ref, ma_ref, pa_ref, cb_ref, mb_ref, pb_ref,
                    qa_ref, qb_ref, ckv_ref, kr_ref, kb_ref, vb_ref, *, qa_rot, qb_rot):
    ca, ma, pa = ca_ref[...], ma_ref[...], pa_ref[...]
    cb, mb, pb = cb_ref[...], mb_ref[...], pb_ref[...]
    scale_a = (NOPE_A + ROPE_A) ** -0.5 * math.log2(math.e)
    scale_b = HD_B ** -0.5 * math.log2(math.e)

    cq = _rms(y_ref[:, 0:Q_LORA], gq_ref[...])
    qa = _dot(cq.astype(BF16), wuq_ref[...].astype(BF16)) * scale_a
    for h in range(H_A):
        base = h * 256
        qa_ref[:, base:base + 128] = qa[:, base:base + 128].astype(qa_ref.dtype)
        qa_ref[:, base + 128:base + 256] = _rope128(
            qa[:, base + 128:base + 256], ca, ma, pa, qa_rot).astype(qa_ref.dtype)

    ckv_ref[...] = _rms(y_ref[:, 512:768], gkv_ref[...])
    kr_ref[...] = _rope128(y_ref[:, 2304:2432], ca, ma, pa, qa_rot)

    for h in range(H_B):
        xb = _rms(y_ref[:, 768 + h * 128:768 + (h + 1) * 128], gqn_ref[...])
        qb_ref[:, h * 128:(h + 1) * 128] = (
            _rope128(xb, cb, mb, pb, qb_rot) * scale_b).astype(qb_ref.dtype)
    for g in range(KV_B):
        xk = _rms(y_ref[:, 1792 + g * 128:1792 + (g + 1) * 128], gkn_ref[...])
        kb_ref[:, g * 128:(g + 1) * 128] = _rope128(xk, cb, mb, pb, qb_rot)
    vb_ref[...] = y_ref[:, 2048:2304]


def _ab_post(y, g_q, g_kv, g_qn, g_kn, w_uq_pad):
    tm = 256
    ca, ma, pa, qa_rot = _rope_tables(ROPE_A, tm)
    cb, mb, pb, qb_rot = _rope_tables(HD_B, tm)
    n_prompt_tiles = NP // tm
    per_seq = SAMPLE_LEN // tm

    def tab_map(i):
        return (jnp.where(i < n_prompt_tiles, 0, 1 + (i - n_prompt_tiles) % per_seq), 0)

    tab = pl.BlockSpec((tm, 128), tab_map)
    row = lambda w: pl.BlockSpec((tm, w), lambda i: (i, 0))
    vec = lambda w: pl.BlockSpec((1, w), lambda i: (0, 0))
    return pl.pallas_call(
        functools.partial(_ab_post_kernel, qa_rot=qa_rot, qb_rot=qb_rot),
        grid=(M_ROWS // tm,),
        in_specs=[row(IN_AB_PAD), vec(Q_LORA), vec(KV_LORA), vec(HD_B), vec(HD_B),
                  pl.BlockSpec((Q_LORA, H_A * 256), lambda i: (0, 0)),
                  tab, tab, tab, tab, tab, tab],
        out_specs=[row(H_A * 256), row(H_B * HD_B), row(KV_LORA), row(128),
                   row(KV_B * HD_B), row(KV_B * HD_B)],
        out_shape=[jax.ShapeDtypeStruct((M_ROWS, H_A * 256), BF16),
                   jax.ShapeDtypeStruct((M_ROWS, H_B * HD_B), BF16),
                   jax.ShapeDtypeStruct((M_ROWS, KV_LORA), F32),
                   jax.ShapeDtypeStruct((M_ROWS, 128), F32),
                   jax.ShapeDtypeStruct((M_ROWS, KV_B * HD_B), F32),
                   jax.ShapeDtypeStruct((M_ROWS, KV_B * HD_B), F32)],
        compiler_params=_params("arbitrary"),
        name="ab_post",
    )(y, g_q.reshape(1, -1), g_kv.reshape(1, -1), g_qn.reshape(1, -1), g_kn.reshape(1, -1),
      w_uq_pad, ca, ma, pa, cb, mb, pb)


def _softmax_pv(s, v):
    m = jnp.max(s, axis=-1, keepdims=True)
    e = jnp.exp2(s - m)
    l = jnp.sum(e, axis=-1, keepdims=True)
    return _dot(e.astype(BF16), v) / l


def _attn_kernel(qa_ref, qb_ref, kv_ref, kr_ref, kb_ref, vb_ref, *rest):
    o_ref = rest[-1]
    kr = kr_ref[...]
    rep = H_B // KV_B
    n_heads = H_A + H_B

    def scores(h):
        if h < H_A:
            k = jnp.concatenate([kv_ref[:, h * 256:h * 256 + 128], kr], axis=1)
            return _dot_nt(qa_ref[:, h * 256:(h + 1) * 256], k)
        g = (h - H_A) // rep
        return _dot_nt(qb_ref[:, (h - H_A) * 128:(h - H_A + 1) * 128], kb_ref[:, g * 128:(g + 1) * 128])

    def values(h):
        if h < H_A:
            return kv_ref[:, h * 256 + 128:(h + 1) * 256]
        g = (h - H_A) // rep
        return vb_ref[:, g * 128:(g + 1) * 128]

    s_next = scores(0)
    for h in range(n_heads):
        s_cur = s_next
        if h + 1 < n_heads:
            s_next = scores(h + 1)
        o_ref[:, h * 128:(h + 1) * 128] = _softmax_pv(s_cur, values(h)).astype(o_ref.dtype)


def _attention(qa, qb, kv, kr, kb, vb, *, n_seq, s_len, t_len, row0, kv_row0, name):
    tq = 256
    nq = s_len // tq
    q0 = row0 // tq
    kv0 = kv_row0 // t_len
    qspec = lambda w: pl.BlockSpec((tq, w), lambda b, i: (q0 + b * nq + i, 0))
    kspec = lambda w, b0=0: pl.BlockSpec((t_len, w), lambda b, i: (b0 + b, 0), pipeline_mode=pl.Buffered(1))
    return pl.pallas_call(
        _attn_kernel,
        grid=(n_seq, nq),
        in_specs=[qspec(H_A * 256), qspec(H_B * HD_B), kspec(H_A * 256, kv0), kspec(128),
                  kspec(KV_B * HD_B), kspec(KV_B * HD_B)],
        out_specs=pl.BlockSpec((tq, D), lambda b, i: (b * nq + i, 0)),
        out_shape=jax.ShapeDtypeStruct((n_seq * s_len, D), BF16),
        compiler_params=_params("arbitrary", "arbitrary"),
        name=name,
    )(qa, qb, kv, kr, kb, vb)


def _row_specs(arr, tm, tile0=0):
    if not isinstance(arr, tuple):
        return [pl.BlockSpec((tm, arr.shape[1]), lambda i, *_: (tile0 + i, 0))], [arr]
    npt = NP // tm
    w = arr[0].shape[1]
    return ([pl.BlockSpec((tm, w), lambda i, *_: (jnp.minimum(tile0 + i, npt - 1), 0)),
             pl.BlockSpec((tm, w), lambda i, *_: (jnp.maximum(tile0 + i - npt, 0), 0))], list(arr))


def _row_load(refs, tile0=0):
    if len(refs) == 1:
        return refs[0][...]
    tm = refs[0].shape[0]
    return jnp.where(tile0 + pl.program_id(0) < NP // tm, refs[0][...], refs[1][...])


def _residual_epilogue(acc, x, gpost_ref, gate_ref, nxt, xo_ref, ho_ref):
    x_new = x + gate_ref[...] * _rms(acc, gpost_ref[...])
    xo_ref[...] = x_new
    if nxt is not None:
        gn_ref, sh_ref, sc_ref = nxt
        ho_ref[...] = (_rms(x_new, gn_ref[...]) * (1.0 + sc_ref[...]) + sh_ref[...]).astype(ho_ref.dtype)


def _proj_res_kernel(*refs, n_a, n_x, emit_h, tile0):
    a_refs, w_ref, x_refs, rest = refs[:n_a], refs[n_a], refs[n_a + 1:n_a + 1 + n_x], refs[n_a + 1 + n_x:]
    if emit_h:
        gpost_ref, gate_ref, gn_ref, sh_ref, sc_ref, xo_ref, ho_ref = rest
        nxt = (gn_ref, sh_ref, sc_ref)
    else:
        gpost_ref, gate_ref, xo_ref = rest
        nxt, ho_ref = None, None
    acc = _dot(_row_load(a_refs, tile0), w_ref[...])
    _residual_epilogue(acc, _row_load(x_refs, tile0), gpost_ref, gate_ref, nxt, xo_ref, ho_ref)


def _res_specs(tm, tile0, n_rows, mod, layer, gate_piece, g_post, nxt):
    vec = pl.BlockSpec((1, D), lambda i, *_: (0, 0))
    specs = [vec, _mod_spec(layer, gate_piece, tm, tile0)]
    args = [g_post.reshape(1, D), mod]
    if nxt is not None:
        g_next, layer_next, sh_piece, sc_piece = nxt
        specs += [vec, _mod_spec(layer_next, sh_piece, tm, tile0), _mod_spec(layer_next, sc_piece, tm, tile0)]
        args += [g_next.reshape(1, D), mod, mod]
    out_specs = [pl.BlockSpec((tm, D), lambda i, *_: (i, 0))]
    out_shape = [jax.ShapeDtypeStruct((n_rows, D), F32)]
    if nxt is not None:
        out_specs.append(pl.BlockSpec((tm, D), lambda i, *_: (i, 0)))
        out_shape.append(jax.ShapeDtypeStruct((n_rows, D), BF16))
    return specs, args, out_specs, out_shape


def _resident_weight_spec(kdim, w_layer):
    return pl.BlockSpec((None, kdim, D), lambda i: (w_layer, 0, 0), pipeline_mode=pl.Buffered(1))


def _proj_res(a, w, w_layer, x, mod, layer, gate_piece, g_post, nxt, *, tm, name, row0=0, n_rows=M_ROWS):
    tile0 = row0 // tm
    a_specs, a_args = _row_specs(a, tm, tile0)
    x_specs, x_args = _row_specs(x, tm, tile0)
    kdim = a_args[0].shape[1]
    specs, args, out_specs, out_shape = _res_specs(tm, tile0, n_rows, mod, layer, gate_piece, g_post, nxt)
    out = pl.pallas_call(
        functools.partial(_proj_res_kernel, n_a=len(a_args), n_x=len(x_args), emit_h=nxt is not None, tile0=tile0),
        grid=(n_rows // tm,),
        in_specs=a_specs + [_resident_weight_spec(kdim, w_layer)] + x_specs + specs,
        out_specs=out_specs,
        out_shape=out_shape,
        compiler_params=_params("arbitrary"),
        name=name,
    )(*a_args, w, *x_args, *args)
    return out if nxt is not None else (out[0], None)


def _ffn_up_kernel(h_ref, hp_ref, hn_ref, wa_ref, wg_ref, cwa_ref, cwg_ref, cba_ref, cbg_ref,
                   mp_ref, mn_ref, o_ref, wbf_ref):
    tm = h_ref.shape[0]
    halo = hp_ref.shape[0]

    @pl.when(pl.program_id(1) == 0)
    def _():
        wbf_ref[0] = wa_ref[...].astype(BF16)
        wbf_ref[1] = wg_ref[...].astype(BF16)

    h = h_ref[...]
    edge = jnp.concatenate([hp_ref[...], hn_ref[...]], axis=0)
    mp = mp_ref[...]
    mn = mn_ref[...]
    row = lax.broadcasted_iota(jnp.int32, (tm, 1), 0)

    def branch(which, cw_ref, cb_ref):
        w = wbf_ref[which]
        u = _dot(h, w)
        u_edge = _dot(edge, w)
        prev = jnp.where(row == 0, u_edge[halo - 1:halo], pltpu.roll(u, 1, 0)) * mp
        nxt = jnp.where(row == tm - 1, u_edge[halo:halo + 1], pltpu.roll(u, tm - 1, 0)) * mn
        cw = cw_ref[...]
        return prev * cw[0:1] + u * cw[1:2] + nxt * cw[2:3] + cb_ref[...]

    a = branch(0, cwa_ref, cba_ref)
    g = branch(1, cwg_ref, cbg_ref)
    o_ref[...] = (g * (1.0 / (1.0 + jnp.exp(-g))) * a).astype(o_ref.dtype)


def _seq_edge_masks():
    pos = np.concatenate([np.tile(np.arange(PROMPT_LEN), N_PROMPT_SEQ),
                          np.tile(np.arange(SAMPLE_LEN), N_SAMPLE_SEQ)])
    lens = np.concatenate([np.full(NP, PROMPT_LEN), np.full(NS, SAMPLE_LEN)])
    has_prev = (pos > 0).astype(np.float32).reshape(M_ROWS, 1)
    has_next = (pos < lens - 1).astype(np.float32).reshape(M_ROWS, 1)
    return jnp.asarray(has_prev), jnp.asarray(has_next)


def _ffn_up(h, w_up, conv_w, conv_b, layer):
    tm, tn, halo = 1024, 512, 16
    nj = D_FF // tn
    r = tm // halo
    last = M_ROWS // halo - 1
    has_prev, has_next = _seq_edge_masks()
    conv_b = conv_b.reshape(conv_b.shape[0], 1, 2 * D_FF)
    col = lambda rows, half: pl.BlockSpec((None, rows, tn), lambda j, i: (layer, 0, j + half * nj))
    return pl.pallas_call(
        _ffn_up_kernel,
        grid=(nj, M_ROWS // tm),
        in_specs=[pl.BlockSpec((tm, D), lambda j, i: (i, 0)),
                  pl.BlockSpec((halo, D), lambda j, i: (jnp.maximum(i * r - 1, 0), 0)),
                  pl.BlockSpec((halo, D), lambda j, i: (jnp.minimum((i + 1) * r, last), 0)),
                  col(D, 0), col(D, 1), col(3, 0), col(3, 1), col(1, 0), col(1, 1),
                  pl.BlockSpec((tm, 1), lambda j, i: (i, 0)),
                  pl.BlockSpec((tm, 1), lambda j, i: (i, 0))],
        out_specs=pl.BlockSpec((tm, tn), lambda j, i: (i, j)),
        out_shape=jax.ShapeDtypeStruct((M_ROWS, D_FF), BF16),
        scratch_shapes=[pltpu.VMEM((2, D, tn), BF16)],
        compiler_params=_params("arbitrary", "arbitrary"),
        name="ffn_up",
    )(h, h, h, w_up, w_up, conv_w, conv_w, conv_b, conv_b, has_prev, has_next)


def _gla_masks(c):
    levels = int(math.log2(c))
    t = np.arange(c)
    pair = [t[:, None] == t[None, :]]
    for l in range(1, levels + 1):
        blk = t >> l
        up = ((t >> (l - 1)) & 1) == 1
        pair.append((blk[:, None] == blk[None, :]) & up[:, None] & ~up[None, :])
    pair = np.stack(pair).astype(np.float32)
    pair = np.stack([pair, pair.transpose(0, 2, 1)])
    mats = []
    for l in range(1, GLA_FINE + 1):
        split = ((t >> l) << l) + (1 << (l - 1)) - 1
        lo, hi = np.minimum(t, split), np.maximum(t, split)
        mats.append((t[None, :] > lo[:, None]) & (t[None, :] <= hi[:, None]))
    same = (t >> GLA_FINE)[:, None] == (t >> GLA_FINE)[None, :]
    mats += [same & (t[None, :] <= t[:, None]), same]
    fine = np.stack([np.concatenate(mats, axis=0),
                     np.concatenate([m[::-1, ::-1] for m in mats], axis=0)]).astype(np.float32)
    return jnp.asarray(pair), jnp.asarray(fine, BF16)


def _gla_chain(q, k, v, za, wb, ba, pair_ref, fine_ref, s_ref, reverse):
    c = q.shape[0]
    levels = pair_ref.shape[1] - 1
    za_hi, za_lo = _split2(za)
    wb_hi, wb_lo = _split2(wb)
    z = _dot(za_hi, wb_hi) + _dot(za_lo, wb_hi) + _dot(za_hi, wb_lo) + ba
    x = (jnp.minimum(z, 0.0) - jnp.log(1.0 + jnp.exp(-jnp.abs(z)))) * (math.log2(math.e) / GATE_TAU)
    yield None

    rev = 1 if reverse else 0
    first_i, second_i = (1, 0) if reverse else (0, 1)
    x_hi, x_lo = _split2(x)
    fine = _dot(fine_ref[rev], jnp.concatenate([x_hi, x_lo], axis=1))
    yield None
    fine = fine[:, :DK_C] + fine[:, DK_C:]
    p, t = fine[GLA_FINE * c:(GLA_FINE + 1) * c], fine[(GLA_FINE + 1) * c:(GLA_FINE + 2) * c]
    k_bf = k.astype(BF16)
    a = pair_ref[rev, 0] * _dot_nt(q.astype(BF16), k_bf)
    for l in range(1, levels + 1):
        s = 1 << (l - 1)
        if l <= GLA_FINE:
            e = jnp.exp2(fine[(l - 1) * c:l * c])
            a = a + pair_ref[rev, l] * _dot_nt((q * e).astype(BF16), (k * e).astype(BF16))
        else:
            nb = c // (2 * s)
            split = lambda arr: arr.reshape(nb, 2, s, arr.shape[-1])
            join = lambda f, sec: jnp.stack([f, sec] if first_i == 0 else [sec, f], axis=1)
            p4, t4 = split(p), split(t)
            p_f, p_s, t_f, t_s = p4[:, first_i], p4[:, second_i], t4[:, first_i], t4[:, second_i]
            q_s = split(q)[:, second_i] * jnp.exp2(p_s)
            k_f = split(k)[:, first_i] * jnp.exp2(t_f - p_f)
            k_all = join(k_f.astype(BF16), split(k_bf)[:, second_i]).reshape(c, DK_C)
            r = _dot_nt(q_s.reshape(c // 2, DK_C).astype(BF16), k_all).reshape(nb, s, c)
            a4 = a.reshape(nb, 2, s, c)
            mask = pair_ref[rev, l].reshape(nb, 2, s, c)[:, second_i]
            a = join(a4[:, first_i], a4[:, second_i] + mask * r).reshape(c, c)
            t_new = t_f + t_s
            p = join(p_f, p_s + t_f).reshape(c, DK_C)
            t = join(t_new, t_new).reshape(c, DK_C)
        yield None
    s_old = s_ref[...]
    o_inter = _dot((q * jnp.exp2(p)).astype(BF16), s_old.astype(BF16))
    kd_t = (k * jnp.exp2(t - p)).T.astype(BF16)
    both = _dot(jnp.concatenate([a.astype(BF16), kd_t], axis=0), v)
    dec_col = jnp.exp2(t[0:8]).T[:, 0:1]
    s_ref[...] = s_old * dec_col + both[c:]
    yield both[:c] + o_inter


def _gla_kernel(*refs, zero_init):
    (qf_ref, kf_ref, vf_ref, zf_ref, qb_ref, kb_ref, vb_ref, zb_ref,
     wbf_ref, baf_ref, wbb_ref, bab_ref, pair_ref, fine_ref) = refs[:14]
    rest = refs[14:]
    if not zero_init:
        sfi_ref, sbi_ref = rest[:2]
        rest = rest[2:]
    of_ref, ob_ref, sfo_ref, sbo_ref, sf_ref, sb_ref = rest
    ci = pl.program_id(1)

    @pl.when(ci == 0)
    def _():
        if zero_init:
            sf_ref[...] = jnp.zeros_like(sf_ref)
            sb_ref[...] = jnp.zeros_like(sb_ref)
        else:
            sf_ref[...] = sfi_ref[...]
            sb_ref[...] = sbi_ref[...]

    scale = DK_C ** -0.5
    chains = []
    for h in range(H_C):
        ks, vs = slice(h * DK_C, (h + 1) * DK_C), slice(h * DV_C, (h + 1) * DV_C)
        chains.append((of_ref, vs, _gla_chain(
            qf_ref[:, ks].astype(F32) * scale, kf_ref[:, ks].astype(F32), vf_ref[:, vs],
            zf_ref[:, 0:GATE_RANK], wbf_ref[:, ks], baf_ref[:, ks], pair_ref, fine_ref, sf_ref.at[h], False)))
        chains.append((ob_ref, vs, _gla_chain(
            qb_ref[:, ks].astype(F32) * scale, kb_ref[:, ks].astype(F32), vb_ref[:, vs],
            zb_ref[:, GATE_RANK:2 * GATE_RANK], wbb_ref[:, ks], bab_ref[:, ks], pair_ref, fine_ref,
            sb_ref.at[h], True)))
    live = True
    while live:
        for o_ref, vs, chain in chains:
            out = next(chain, "done")
            if isinstance(out, str):
                live = False
            elif out is not None:
                o_ref[:, vs] = out.astype(o_ref.dtype)

    @pl.when(ci == pl.num_programs(1) - 1)
    def _():
        sfo_ref[...] = sf_ref[...]
        sbo_ref[...] = sb_ref[...]


def _gla(y, z, wb_f, ba_f, wb_b, ba_b, s_f, s_b, *, n_seq, s_len, row0, name):
    c = GLA_CHUNK
    n = s_len // c
    r0 = row0 // c
    zero_init = s_f is None
    pair, fine = _gla_masks(c)
    fwd = lambda b, i: b * n + i
    bwd = lambda b, i: b * n + (n - 1 - i)

    def dir_specs(rows):
        return [pl.BlockSpec((c, HK_C), lambda b, i: (r0 + rows(b, i), 0)),
                pl.BlockSpec((c, HK_C), lambda b, i: (r0 + rows(b, i), 1)),
                pl.BlockSpec((c, HV_C), lambda b, i: (r0 + rows(b, i), 1)),
                pl.BlockSpec((c, 128), lambda b, i: (r0 + rows(b, i), 0))]

    full = lambda arr: pl.BlockSpec(arr.shape, lambda b, i: (0,) * arr.ndim)
    state = pl.BlockSpec((None, H_C, DK_C, DV_C), lambda b, i: (b, 0, 0, 0))
    ba_f, ba_b = ba_f.reshape(1, -1), ba_b.reshape(1, -1)
    in_specs = (dir_specs(fwd) + dir_specs(bwd)
                + [full(wb_f), full(ba_f), full(wb_b), full(ba_b), full(pair), full(fine)])
    args = [y, y, y, z, y, y, y, z, wb_f, ba_f, wb_b, ba_b, pair, fine]
    if not zero_init:
        in_specs += [state, state]
        args += [s_f, s_b]
    return pl.pallas_call(
        functools.partial(_gla_kernel, zero_init=zero_init),
        grid=(n_seq, n),
        in_specs=in_specs,
        out_specs=[pl.BlockSpec((c, HV_C), lambda b, i: (fwd(b, i), 0)),
                   pl.BlockSpec((c, HV_C), lambda b, i: (bwd(b, i), 0)),
                   state, state],
        out_shape=[jax.ShapeDtypeStruct((n_seq * s_len, HV_C), BF16),
                   jax.ShapeDtypeStruct((n_seq * s_len, HV_C), BF16),
                   jax.ShapeDtypeStruct((n_seq, H_C, DK_C, DV_C), F32),
                   jax.ShapeDtypeStruct((n_seq, H_C, DK_C, DV_C), F32)],
        scratch_shapes=[pltpu.VMEM((H_C, DK_C, DV_C), F32), pltpu.VMEM((H_C, DK_C, DV_C), F32)],
        compiler_params=_params("arbitrary", "arbitrary"),
        name=name,
    )(*args)


def _gla_out_kernel(ofp_ref, ofs_ref, obp_ref, obs_ref, r_ref, go_ref, w_ref, x_ref, gpost_ref, gate_ref,
                    gn_ref, sh_ref, sc_ref, xo_ref, ho_ref, a_ref):
    o_all = _row_load([ofp_ref, ofs_ref]).astype(F32) + _row_load([obp_ref, obs_ref]).astype(F32)
    for h in range(H_C):
        vs = slice(h * DV_C, (h + 1) * DV_C)
        r = r_ref[:, vs].astype(F32)
        a = _rms(o_all[:, vs], go_ref[...]) * (r * (1.0 / (1.0 + jnp.exp(-r))))
        a_ref[:, vs] = a.astype(BF16)
    _residual_epilogue(_dot(a_ref[...], w_ref[...]), x_ref[...], gpost_ref, gate_ref,
                       (gn_ref, sh_ref, sc_ref), xo_ref, ho_ref)


def _gla_out(o_f, o_b, y, g_out, w, w_layer, x, mod, layer, g_post, nxt):
    tm = 256
    r_blk = (2 * HK_C + HV_C) // HV_C
    specs, args, out_specs, out_shape = _res_specs(tm, 0, M_ROWS, mod, layer, 2, g_post, nxt)
    of_specs, of_args = _row_specs(o_f, tm)
    ob_specs, ob_args = _row_specs(o_b, tm)
    return pl.pallas_call(
        _gla_out_kernel,
        grid=(M_ROWS // tm,),
        in_specs=of_specs + ob_specs + [
            pl.BlockSpec((tm, HV_C), lambda i: (i, r_blk)),
            pl.BlockSpec((1, DV_C), lambda i: (0, 0)),
            _resident_weight_spec(HV_C, w_layer),
            pl.BlockSpec((tm, D), lambda i: (i, 0))] + specs,
        out_specs=out_specs,
        out_shape=out_shape,
        scratch_shapes=[pltpu.VMEM((tm, HV_C), BF16)],
        compiler_params=_params("arbitrary"),
        name="gla_out",
    )(*of_args, *ob_args, y, g_out.reshape(1, -1), w, x, *args)


def _with_cache(cache, new):
    w = new.shape[-1]
    cat = jnp.concatenate([cache.reshape(N_SAMPLE_SEQ, PAST, w).astype(BF16),
                           new.reshape(N_SAMPLE_SEQ, SAMPLE_LEN, w).astype(BF16)], axis=1)
    return cat.reshape(N_SAMPLE_SEQ * (PAST + SAMPLE_LEN), w)


def kernel(x_prompt, x_sample, cache_mla_ckv, cache_mla_krope, cache_gqa_k, cache_gqa_v, state_gla_fwd, state_gla_bwd, c, c_ctx, w_mod, b_mod, g_pre_mix, g_post_mix, g_pre_ffn, g_post_ffn, w_in_ab, g_mla_q, g_mla_kv, w_mla_uq, w_mla_ukv, g_gqa_q, g_gqa_k, w_out_ab, w_in_c, w_gate_fwd_a, w_gate_fwd_b, b_gate_fwd, w_gate_bwd_a, w_gate_bwd_b, b_gate_bwd, g_gla_out, w_out_c, w_ffn_up, ffn_conv_w, ffn_conv_b, w_ffn_down):
    depth = w_mod.shape[0]
    cond = jnp.concatenate([c_ctx[None], c, jnp.zeros((N_MOD_ROWS - 1 - N_SAMPLE_SEQ, D), F32)], axis=0)
    mod = _modulation(cond, w_mod, b_mod).reshape(depth * N_MOD_ROWS * 6, 1, D)
    w_out_ab_bf, w_out_c_bf, w_down_bf = w_out_ab.astype(BF16), w_out_c.astype(BF16), w_ffn_down.astype(BF16)
    t_sample = PAST + SAMPLE_LEN

    x = (x_prompt.reshape(NP, D), x_sample.reshape(NS, D))
    h = _prenorm(x, g_pre_mix[0], mod, 0)
    caches, states = [], []
    for i in range(depth):
        j = i // 2
        ffn_pre = (g_pre_ffn[i], i, 3, 4)
        if i % 2 == 0:
            w = w_in_ab[j]
            w_in = jnp.concatenate(
                [w[:, 0:768], w[:, 832:2368], w[:, 768:832], jnp.zeros((D, IN_AB_PAD - 2368), F32)], axis=1)
            wq = w_mla_uq[j].reshape(Q_LORA, H_A, NOPE_A + ROPE_A)
            w_uq = jnp.concatenate([wq, jnp.zeros((Q_LORA, H_A, 256 - NOPE_A - ROPE_A), F32)],
                                   axis=2).reshape(Q_LORA, H_A * 256)
            y = _matmul(h, w_in, tm=1024, tn=512, out_dtype=F32, name="in_ab")
            qa, qb, ckv, kr, kb, vb = _ab_post(y, g_mla_q[j], g_mla_kv[j], g_gqa_q[j], g_gqa_k[j], w_uq)
            ckv_all = jnp.concatenate([_with_cache(cache_mla_ckv[:, j], ckv[NP:]), ckv[:NP].astype(BF16)], axis=0)
            kv = _matmul(ckv_all, w_mla_ukv[j], tm=1024, tn=1024, out_dtype=BF16, name="kv_up")
            o_p = _attention(qa, qb, kv, kr[:NP].astype(BF16), kb[:NP].astype(BF16), vb[:NP].astype(BF16),
                             n_seq=N_PROMPT_SEQ, s_len=PROMPT_LEN, t_len=PROMPT_LEN, row0=0,
                             kv_row0=N_SAMPLE_SEQ * t_sample, name="attn_prompt")
            kr_cache = jnp.pad(cache_mla_krope[:, j], ((0, 0), (0, 0), (0, 128 - ROPE_A)))
            o_s = _attention(qa, qb, kv, _with_cache(kr_cache, kr[NP:]),
                             _with_cache(cache_gqa_k[:, j], kb[NP:]), _with_cache(cache_gqa_v[:, j], vb[NP:]),
                             n_seq=N_SAMPLE_SEQ, s_len=SAMPLE_LEN, t_len=t_sample, row0=NP, kv_row0=0,
                             name="attn_sample")
            x, h = _proj_res((o_p, o_s), w_out_ab_bf, j, x, mod, i, 2, g_post_mix[i], ffn_pre, tm=512,
                             name="out_ab")
            caches.append((ckv[:NP].reshape(N_PROMPT_SEQ, PROMPT_LEN, KV_LORA),
                           kr[:NP, :ROPE_A].reshape(N_PROMPT_SEQ, PROMPT_LEN, ROPE_A),
                           kb[:NP].reshape(N_PROMPT_SEQ, PROMPT_LEN, KV_B, HD_B),
                           vb[:NP].reshape(N_PROMPT_SEQ, PROMPT_LEN, KV_B, HD_B)))
        else:
            y = _matmul(h, w_in_c[j], tm=1024, tn=768, out_dtype=BF16, name="in_c")
            w_z = jnp.concatenate([w_gate_fwd_a[j], w_gate_bwd_a[j], jnp.zeros((D, 128 - 2 * GATE_RANK), F32)], axis=1)
            z = _matmul(h, w_z, tm=1024, tn=128, out_dtype=F32, name="gate_lowrank")
            gate_args = (w_gate_fwd_b[j], b_gate_fwd[j], w_gate_bwd_b[j], b_gate_bwd[j])
            of_p, ob_p, sf, sb = _gla(y, z, *gate_args, None, None,
                                      n_seq=N_PROMPT_SEQ, s_len=PROMPT_LEN, row0=0, name="gla_prompt")
            of_s, ob_s, _, _ = _gla(y, z, *gate_args, state_gla_fwd[:, j], state_gla_bwd[:, j],
                                    n_seq=N_SAMPLE_SEQ, s_len=SAMPLE_LEN, row0=NP, name="gla_sample")
            x, h = _gla_out((of_p, of_s), (ob_p, ob_s), y, g_gla_out[j], w_out_c_bf, j, x, mod, i,
                            g_post_mix[i], ffn_pre)
            states.append((sf, sb))
        act = _ffn_up(h, w_ffn_up, ffn_conv_w, ffn_conv_b, i)
        if i + 1 < depth:
            nxt = (g_pre_mix[i + 1], i + 1, 0, 1)
            x, h = _proj_res(act, w_down_bf, i, x, mod, i, 5, g_post_ffn[i], nxt, tm=256, name="ffn_down")
        else:
            y_prompt, _ = _proj_res(act, w_down_bf, i, x, mod, i, 5, g_post_ffn[i], None, tm=256,
                                    name="ffn_down_prompt", row0=0, n_rows=NP)
            y_sample, _ = _proj_res(act, w_down_bf, i, x, mod, i, 5, g_post_ffn[i], None, tm=256,
                                    name="ffn_down_sample", row0=NP, n_rows=NS)

    new_ckv, new_krope, new_k, new_v = (jnp.stack(t, axis=1) for t in zip(*caches))
    new_sf, new_sb = (jnp.stack(t, axis=1) for t in zip(*states))
    return (y_prompt.reshape(N_PROMPT_SEQ, PROMPT_LEN, D), y_sample.reshape(N_SAMPLE_SEQ, SAMPLE_LEN, D),
            new_ckv, new_krope, new_k, new_v, new_sf, new_sb)
```

```python
import functools
import math

import numpy as np
import jax
import jax.numpy as jnp
from jax import lax
from jax.experimental import pallas as pl
from jax.experimental.pallas import tpu as pltpu

D = 2048
N_PROMPT_SEQ, PROMPT_LEN = 16, 256
N_SAMPLE_SEQ, SAMPLE_LEN = 4, 2048
PAST = 512
GRID_W = 64
NP = N_PROMPT_SEQ * PROMPT_LEN
NS = N_SAMPLE_SEQ * SAMPLE_LEN
M_ROWS = NP + NS
EPS = 1e-6
ROPE_THETA = 10000.0

H_A, NOPE_A, ROPE_A, V_A = 8, 128, 64, 128
Q_LORA, KV_LORA = 512, 256
H_B, KV_B, HD_B = 8, 2, 128
IN_AB_PAD = 2560
H_C, DK_C, DV_C = 4, 256, 512
HK_C, HV_C = H_C * DK_C, H_C * DV_C
GATE_RANK = 16
GATE_TAU = 16.0
D_FF = 5632
GLA_CHUNK = 128
GLA_FINE = 3
N_MOD_ROWS = 8

VMEM_LIMIT_BYTES = 56 * 1024 * 1024
BF16 = jnp.bfloat16
F32 = jnp.float32


def _params(*sem):
    return pltpu.CompilerParams(dimension_semantics=sem, vmem_limit_bytes=VMEM_LIMIT_BYTES)


def _mod_row(i, tm):
    n_prompt_tiles = NP // tm
    per_seq = SAMPLE_LEN // tm
    return jnp.where(i < n_prompt_tiles, 0, 1 + (i - n_prompt_tiles) // per_seq)


def _mod_spec(layer, piece, tm, tile0=0):
    def index_map(i, *_):
        return ((layer * N_MOD_ROWS + _mod_row(tile0 + i, tm)) * 6 + piece, 0, 0)
    return pl.BlockSpec((None, 1, D), index_map)


def _rms(x, g):
    return x * lax.rsqrt(jnp.mean(x * x, axis=-1, keepdims=True) + EPS) * g


def _dot(a, b):
    return jnp.dot(a, b, preferred_element_type=F32)


def _dot_nt(a, b):
    return lax.dot_general(a, b, (((1,), (1,)), ((), ())), preferred_element_type=F32)


def _split2(x):
    hi = x.astype(BF16)
    lo = (x - hi.astype(F32)).astype(BF16)
    return hi, lo


def _mod_kernel(c_ref, w_ref, b_ref, o_ref):
    c = c_ref[...]
    a = c * (1.0 / (1.0 + jnp.exp(-c)))
    a_hi, a_lo = _split2(a)
    w_hi, w_lo = _split2(w_ref[...])
    r = _dot(jnp.concatenate([a_hi, a_lo], axis=0), w_hi)
    o_ref[...] = r[:N_MOD_ROWS] + r[N_MOD_ROWS:] + _dot(a_hi, w_lo) + b_ref[...]


def _modulation(cond, w_mod, b_mod):
    depth = w_mod.shape[0]
    n = w_mod.shape[2]
    tn = 1024
    return pl.pallas_call(
        _mod_kernel,
        grid=(depth, n // tn),
        in_specs=[pl.BlockSpec((N_MOD_ROWS, D), lambda l, j: (0, 0)),
                  pl.BlockSpec((None, D, tn), lambda l, j: (l, 0, j)),
                  pl.BlockSpec((None, 1, tn), lambda l, j: (l, 0, j))],
        out_specs=pl.BlockSpec((None, N_MOD_ROWS, tn), lambda l, j: (l, 0, j)),
        out_shape=jax.ShapeDtypeStruct((depth, N_MOD_ROWS, n), F32),
        compiler_params=_params("arbitrary", "arbitrary"),
        name="modulation",
    )(cond, w_mod, b_mod.reshape(depth, 1, n))


def _prenorm_kernel(xp_ref, xs_ref, g_ref, sh_ref, sc_ref, h_ref):
    x = _row_load([xp_ref, xs_ref])
    h_ref[...] = (_rms(x, g_ref[...]) * (1.0 + sc_ref[...]) + sh_ref[...]).astype(h_ref.dtype)


def _prenorm(x_pair, g, mod, layer):
    tm = 512
    x_specs, x_args = _row_specs(x_pair, tm)
    return pl.pallas_call(
        _prenorm_kernel,
        grid=(M_ROWS // tm,),
        in_specs=x_specs + [pl.BlockSpec((1, D), lambda i: (0, 0)),
                            _mod_spec(layer, 0, tm), _mod_spec(layer, 1, tm)],
        out_specs=pl.BlockSpec((tm, D), lambda i: (i, 0)),
        out_shape=jax.ShapeDtypeStruct((M_ROWS, D), BF16),
        compiler_params=_params("arbitrary"),
        name="prenorm",
    )(*x_args, g.reshape(1, D), mod, mod)


def _matmul_kernel(x_ref, w_ref, o_ref):
    o_ref[...] = _dot(x_ref[...].astype(BF16), w_ref[...].astype(BF16)).astype(o_ref.dtype)


def _matmul(x, w, *, tm, tn, out_dtype, name):
    m, k = x.shape
    n = w.shape[1]
    return pl.pallas_call(
        _matmul_kernel,
        grid=(n // tn, m // tm),
        in_specs=[pl.BlockSpec((tm, k), lambda j, i: (i, 0)),
                  pl.BlockSpec((k, tn), lambda j, i: (0, j))],
        out_specs=pl.BlockSpec((tm, tn), lambda j, i: (i, j)),
        out_shape=jax.ShapeDtypeStruct((m, n), out_dtype),
        compiler_params=_params("arbitrary", "arbitrary"),
        name=name,
    )(x, w)


def _rope_tables(rot_dim, tm):
    q = rot_dim // 4
    inv = ROPE_THETA ** (-np.arange(q, dtype=np.float64) / q)
    tok = np.arange(SAMPLE_LEN)
    ang_r = (tok // GRID_W)[:, None] * inv
    ang_c = (tok % GRID_W)[:, None] * inv
    ang = np.concatenate([ang_r, ang_r, ang_c, ang_c], axis=-1)
    cos, sin = np.cos(ang), np.sin(ang)
    first = (np.arange(rot_dim) % (2 * q)) < q
    sin_m = np.where(first, -sin, 0.0)
    sin_p = np.where(first, 0.0, sin)

    def widen(t, ident):
        full = np.full((tm + SAMPLE_LEN, 128), ident, np.float64)
        full[tm:, :rot_dim] = t
        return jnp.asarray(full, F32)

    return widen(cos, 1.0), widen(sin_m, 0.0), widen(sin_p, 0.0), q


def _rope128(x, cos, sin_m, sin_p, q):
    return (x * cos + pltpu.roll(x, 128 - q, 1) * sin_m + pltpu.roll(x, q, 1) * sin_p)


def _ab_post_kernel(y_ref, gq_ref, gkv_ref, gqn_ref, gkn_ref, wuq_ref,
                    ca_ref, ma_ref, pa_ref, cb_ref, mb_ref, pb_ref,
                    qa_ref, qb_ref, ckv_ref, kr_ref, kb_ref, vb_ref, *, qa_rot, qb_rot):
    ca, ma, pa = ca_ref[...], ma_ref[...], pa_ref[...]
    cb, mb, pb = cb_ref[...], mb_ref[...], pb_ref[...]
    scale_a = (NOPE_A + ROPE_A) ** -0.5 * math.log2(math.e)
    scale_b = HD_B ** -0.5 * math.log2(math.e)

    cq = _rms(y_ref[:, 0:Q_LORA], gq_ref[...])
    qa = _dot(cq.astype(BF16), wuq_ref[...].astype(BF16)) * scale_a
    for h in range(H_A):
        base = h * 256
        qa_ref[:, base:base + 128] = qa[:, base:base + 128].astype(qa_ref.dtype)
        qa_ref[:, base + 128:base + 256] = _rope128(
            qa[:, base + 128:base + 256], ca, ma, pa, qa_rot).astype(qa_ref.dtype)

    ckv_ref[...] = _rms(y_ref[:, 512:768], gkv_ref[...])
    kr_ref[...] = _rope128(y_ref[:, 2304:2432], ca, ma, pa, qa_rot)

    for h in range(H_B):
        xb = _rms(y_ref[:, 768 + h * 128:768 + (h + 1) * 128], gqn_ref[...])
        qb_ref[:, h * 128:(h + 1) * 128] = (
            _rope128(xb, cb, mb, pb, qb_rot) * scale_b).astype(qb_ref.dtype)
    for g in range(KV_B):
        xk = _rms(y_ref[:, 1792 + g * 128:1792 + (g + 1) * 128], gkn_ref[...])
        kb_ref[:, g * 128:(g + 1) * 128] = _rope128(xk, cb, mb, pb, qb_rot)
    vb_ref[...] = y_ref[:, 2048:2304]


def _ab_post(y, g_q, g_kv, g_qn, g_kn, w_uq_pad):
    tm = 256
    ca, ma, pa, qa_rot = _rope_tables(ROPE_A, tm)
    cb, mb, pb, qb_rot = _rope_tables(HD_B, tm)
    n_prompt_tiles = NP // tm
    per_seq = SAMPLE_LEN // tm

    def tab_map(i):
        return (jnp.where(i < n_prompt_tiles, 0, 1 + (i - n_prompt_tiles) % per_seq), 0)

    tab = pl.BlockSpec((tm, 128), tab_map)
    row = lambda w: pl.BlockSpec((tm, w), lambda i: (i, 0))
    vec = lambda w: pl.BlockSpec((1, w), lambda i: (0, 0))
    return pl.pallas_call(
        functools.partial(_ab_post_kernel, qa_rot=qa_rot, qb_rot=qb_rot),
        grid=(M_ROWS // tm,),
        in_specs=[row(IN_AB_PAD), vec(Q_LORA), vec(KV_LORA), vec(HD_B), vec(HD_B),
                  pl.BlockSpec((Q_LORA, H_A * 256), lambda i: (0, 0)),
                  tab, tab, tab, tab, tab, tab],
        out_specs=[row(H_A * 256), row(H_B * HD_B), row(KV_LORA), row(128),
                   row(KV_B * HD_B), row(KV_B * HD_B)],
        out_shape=[jax.ShapeDtypeStruct((M_ROWS, H_A * 256), BF16),
                   jax.ShapeDtypeStruct((M_ROWS, H_B * HD_B), BF16),
                   jax.ShapeDtypeStruct((M_ROWS, KV_LORA), F32),
                   jax.ShapeDtypeStruct((M_ROWS, 128), F32),
                   jax.ShapeDtypeStruct((M_ROWS, KV_B * HD_B), F32),
                   jax.ShapeDtypeStruct((M_ROWS, KV_B * HD_B), F32)],
        compiler_params=_params("arbitrary"),
        name="ab_post",
    )(y, g_q.reshape(1, -1), g_kv.reshape(1, -1), g_qn.reshape(1, -1), g_kn.reshape(1, -1),
      w_uq_pad, ca, ma, pa, cb, mb, pb)


def _softmax_pv(s, v):
    m = jnp.max(s, axis=-1, keepdims=True)
    e = jnp.exp2(s - m)
    l = jnp.sum(e, axis=-1, keepdims=True)
    return _dot(e.astype(BF16), v) / l


def _attn_kernel(qa_ref, qb_ref, kv_ref, kr_ref, kb_ref, vb_ref, *rest):
    o_ref = rest[-1]
    kr = kr_ref[...]
    rep = H_B // KV_B
    n_heads = H_A + H_B

    def scores(h):
        if h < H_A:
            k = jnp.concatenate([kv_ref[:, h * 256:h * 256 + 128], kr], axis=1)
            return _dot_nt(qa_ref[:, h * 256:(h + 1) * 256], k)
        g = (h - H_A) // rep
        return _dot_nt(qb_ref[:, (h - H_A) * 128:(h - H_A + 1) * 128], kb_ref[:, g * 128:(g + 1) * 128])

    def values(h):
        if h < H_A:
            return kv_ref[:, h * 256 + 128:(h + 1) * 256]
        g = (h - H_A) // rep
        return vb_ref[:, g * 128:(g + 1) * 128]

    s_next = scores(0)
    for h in range(n_heads):
        s_cur = s_next
        if h + 1 < n_heads:
            s_next = scores(h + 1)
        o_ref[:, h * 128:(h + 1) * 128] = _softmax_pv(s_cur, values(h)).astype(o_ref.dtype)


def _attention(qa, qb, kv, kr, kb, vb, *, n_seq, s_len, t_len, row0, kv_row0, name):
    tq = 256
    nq = s_len // tq
    q0 = row0 // tq
    kv0 = kv_row0 // t_len
    qspec = lambda w: pl.BlockSpec((tq, w), lambda b, i: (q0 + b * nq + i, 0))
    kspec = lambda w, b0=0: pl.BlockSpec((t_len, w), lambda b, i: (b0 + b, 0), pipeline_mode=pl.Buffered(1))
    return pl.pallas_call(
        _attn_kernel,
        grid=(n_seq, nq),
        in_specs=[qspec(H_A * 256), qspec(H_B * HD_B), kspec(H_A * 256, kv0), kspec(128),
                  kspec(KV_B * HD_B), kspec(KV_B * HD_B)],
        out_specs=pl.BlockSpec((tq, D), lambda b, i: (b * nq + i, 0)),
        out_shape=jax.ShapeDtypeStruct((n_seq * s_len, D), BF16),
        compiler_params=_params("arbitrary", "arbitrary"),
        name=name,
    )(qa, qb, kv, kr, kb, vb)


def _row_specs(arr, tm, tile0=0):
    if not isinstance(arr, tuple):
        return [pl.BlockSpec((tm, arr.shape[1]), lambda i, *_: (tile0 + i, 0))], [arr]
    npt = NP // tm
    w = arr[0].shape[1]
    return ([pl.BlockSpec((tm, w), lambda i, *_: (jnp.minimum(tile0 + i, npt - 1), 0)),
             pl.BlockSpec((tm, w), lambda i, *_: (jnp.maximum(tile0 + i - npt, 0), 0))], list(arr))


def _row_load(refs, tile0=0):
    if len(refs) == 1:
        return refs[0][...]
    tm = refs[0].shape[0]
    return jnp.where(tile0 + pl.program_id(0) < NP // tm, refs[0][...], refs[1][...])


def _residual_epilogue(acc, x, gpost_ref, gate_ref, nxt, xo_ref, ho_ref):
    x_new = x + gate_ref[...] * _rms(acc, gpost_ref[...])
    xo_ref[...] = x_new
    if nxt is not None:
        gn_ref, sh_ref, sc_ref = nxt
        ho_ref[...] = (_rms(x_new, gn_ref[...]) * (1.0 + sc_ref[...]) + sh_ref[...]).astype(ho_ref.dtype)


def _proj_res_kernel(*refs, n_a, n_x, emit_h, tile0):
    a_refs, w_ref, x_refs, rest = refs[:n_a], refs[n_a], refs[n_a + 1:n_a + 1 + n_x], refs[n_a + 1 + n_x:]
    if emit_h:
        gpost_ref, gate_ref, gn_ref, sh_ref, sc_ref, xo_ref, ho_ref = rest
        nxt = (gn_ref, sh_ref, sc_ref)
    else:
        gpost_ref, gate_ref, xo_ref = rest
        nxt, ho_ref = None, None
    acc = _dot(_row_load(a_refs, tile0), w_ref[...])
    _residual_epilogue(acc, _row_load(x_refs, tile0), gpost_ref, gate_ref, nxt, xo_ref, ho_ref)


def _res_specs(tm, tile0, n_rows, mod, layer, gate_piece, g_post, nxt):
    vec = pl.BlockSpec((1, D), lambda i, *_: (0, 0))
    specs = [vec, _mod_spec(layer, gate_piece, tm, tile0)]
    args = [g_post.reshape(1, D), mod]
    if nxt is not None:
        g_next, layer_next, sh_piece, sc_piece = nxt
        specs += [vec, _mod_spec(layer_next, sh_piece, tm, tile0), _mod_spec(layer_next, sc_piece, tm, tile0)]
        args += [g_next.reshape(1, D), mod, mod]
    out_specs = [pl.BlockSpec((tm, D), lambda i, *_: (i, 0))]
    out_shape = [jax.ShapeDtypeStruct((n_rows, D), F32)]
    if nxt is not None:
        out_specs.append(pl.BlockSpec((tm, D), lambda i, *_: (i, 0)))
        out_shape.append(jax.ShapeDtypeStruct((n_rows, D), BF16))
    return specs, args, out_specs, out_shape


def _resident_weight_spec(kdim, w_layer):
    return pl.BlockSpec((None, kdim, D), lambda i: (w_layer, 0, 0), pipeline_mode=pl.Buffered(1))


def _proj_res(a, w, w_layer, x, mod, layer, gate_piece, g_post, nxt, *, tm, name, row0=0, n_rows=M_ROWS):
    tile0 = row0 // tm
    a_specs, a_args = _row_specs(a, tm, tile0)
    x_specs, x_args = _row_specs(x, tm, tile0)
    kdim = a_args[0].shape[1]
    specs, args, out_specs, out_shape = _res_specs(tm, tile0, n_rows, mod, layer, gate_piece, g_post, nxt)
    out = pl.pallas_call(
        functools.partial(_proj_res_kernel, n_a=len(a_args), n_x=len(x_args), emit_h=nxt is not None, tile0=tile0),
        grid=(n_rows // tm,),
        in_specs=a_specs + [_resident_weight_spec(kdim, w_layer)] + x_specs + specs,
        out_specs=out_specs,
        out_shape=out_shape,
        compiler_params=_params("arbitrary"),
        name=name,
    )(*a_args, w, *x_args, *args)
    return out if nxt is not None else (out[0], None)


def _ffn_up_kernel(h_ref, hp_ref, hn_ref, wa_ref, wg_ref, cwa_ref, cwg_ref, cba_ref, cbg_ref,
                   mp_ref, mn_ref, o_ref, wbf_ref):
    tm = h_ref.shape[0]
    halo = hp_ref.shape[0]

    @pl.when(pl.program_id(1) == 0)
    def _():
        wbf_ref[0] = wa_ref[...].astype(BF16)
        wbf_ref[1] = wg_ref[...].astype(BF16)

    h = h_ref[...]
    edge = jnp.concatenate([hp_ref[...], hn_ref[...]], axis=0)
    row8 = lax.broadcasted_iota(jnp.int32, (8, 1), 0)
    n_seg = tm // PROMPT_LEN

    def patch_edges(shifted, tile_edge_row, at_start, mask_ref):
        pieces = []
        for seg in range(n_seg):
            r = seg * PROMPT_LEN if at_start else (seg + 1) * PROMPT_LEN - 1
            g0 = r - r % 8
            grp = shifted[g0:g0 + 8]
            at_tile_edge = seg == 0 if at_start else seg == n_seg - 1
            src = tile_edge_row if at_tile_edge else grp
            grp = jnp.where(row8 == r % 8, src * mask_ref[r:r + 1, :], grp)
            if at_start:
                pieces += [grp, shifted[g0 + 8:(seg + 1) * PROMPT_LEN]]
            else:
                pieces += [shifted[seg * PROMPT_LEN:g0], grp]
        return jnp.concatenate(pieces, axis=0)

    def branch(which, cw_ref, cb_ref):
        w = wbf_ref[which]
        u = _dot(h, w)
        u_edge = _dot(edge, w)
        prev = patch_edges(pltpu.roll(u, 1, 0), u_edge[halo - 1:halo], True, mp_ref)
        nxt = patch_edges(pltpu.roll(u, tm - 1, 0), u_edge[halo:halo + 1], False, mn_ref)
        cw = cw_ref[...]
        return prev * cw[0:1] + u * cw[1:2] + nxt * cw[2:3] + cb_ref[...]

    a = branch(0, cwa_ref, cba_ref)
    g = branch(1, cwg_ref, cbg_ref)
    o_ref[...] = (g * (1.0 / (1.0 + jnp.exp(-g))) * a).astype(o_ref.dtype)


def _seq_edge_masks():
    pos = np.concatenate([np.tile(np.arange(PROMPT_LEN), N_PROMPT_SEQ),
                          np.tile(np.arange(SAMPLE_LEN), N_SAMPLE_SEQ)])
    lens = np.concatenate([np.full(NP, PROMPT_LEN), np.full(NS, SAMPLE_LEN)])
    has_prev = (pos > 0).astype(np.float32).reshape(M_ROWS, 1)
    has_next = (pos < lens - 1).astype(np.float32).reshape(M_ROWS, 1)
    return jnp.asarray(has_prev), jnp.asarray(has_next)


def _ffn_up(h, w_up, conv_w, conv_b, layer):
    tm, tn, halo = 1024, 512, 16
    nj = D_FF // tn
    r = tm // halo
    last = M_ROWS // halo - 1
    has_prev, has_next = _seq_edge_masks()
    conv_b = conv_b.reshape(conv_b.shape[0], 1, 2 * D_FF)
    col = lambda rows, half: pl.BlockSpec((None, rows, tn), lambda j, i: (layer, 0, j + half * nj))
    return pl.pallas_call(
        _ffn_up_kernel,
        grid=(nj, M_ROWS // tm),
        in_specs=[pl.BlockSpec((tm, D), lambda j, i: (i, 0)),
                  pl.BlockSpec((halo, D), lambda j, i: (jnp.maximum(i * r - 1, 0), 0)),
                  pl.BlockSpec((halo, D), lambda j, i: (jnp.minimum((i + 1) * r, last), 0)),
                  col(D, 0), col(D, 1), col(3, 0), col(3, 1), col(1, 0), col(1, 1),
                  pl.BlockSpec((tm, 1), lambda j, i: (i, 0)),
                  pl.BlockSpec((tm, 1), lambda j, i: (i, 0))],
        out_specs=pl.BlockSpec((tm, tn), lambda j, i: (i, j)),
        out_shape=jax.ShapeDtypeStruct((M_ROWS, D_FF), BF16),
        scratch_shapes=[pltpu.VMEM((2, D, tn), BF16)],
        compiler_params=_params("arbitrary", "arbitrary"),
        name="ffn_up",
    )(h, h, h, w_up, w_up, conv_w, conv_w, conv_b, conv_b, has_prev, has_next)


def _gla_masks(c):
    levels = int(math.log2(c))
    t = np.arange(c)
    pair = [t[:, None] == t[None, :]]
    for l in range(1, levels + 1):
        blk = t >> l
        up = ((t >> (l - 1)) & 1) == 1
        pair.append((blk[:, None] == blk[None, :]) & up[:, None] & ~up[None, :])
    pair = np.stack(pair).astype(np.float32)
    pair = np.stack([pair, pair.transpose(0, 2, 1)])
    mats = []
    for l in range(1, GLA_FINE + 1):
        split = ((t >> l) << l) + (1 << (l - 1)) - 1
        lo, hi = np.minimum(t, split), np.maximum(t, split)
        mats.append((t[None, :] > lo[:, None]) & (t[None, :] <= hi[:, None]))
    same = (t >> GLA_FINE)[:, None] == (t >> GLA_FINE)[None, :]
    mats += [same & (t[None, :] <= t[:, None]), same]
    fine = np.stack([np.concatenate(mats, axis=0),
                     np.concatenate([m[::-1, ::-1] for m in mats], axis=0)]).astype(np.float32)
    return jnp.asarray(pair), jnp.asarray(fine, BF16)


def _gla_chain(q, k, v, za, wb, ba, pair_ref, fine_ref, s_ref, reverse):
    c = q.shape[0]
    levels = pair_ref.shape[1] - 1
    za_hi, za_lo = _split2(za)
    wb_hi, wb_lo = _split2(wb)
    z = _dot(za_hi, wb_hi) + _dot(za_lo, wb_hi) + _dot(za_hi, wb_lo) + ba
    x = (jnp.minimum(z, 0.0) - jnp.log(1.0 + jnp.exp(-jnp.abs(z)))) * (math.log2(math.e) / GATE_TAU)
    yield None

    rev = 1 if reverse else 0
    first_i, second_i = (1, 0) if reverse else (0, 1)
    x_hi, x_lo = _split2(x)
    fine = _dot(fine_ref[rev], jnp.concatenate([x_hi, x_lo], axis=1))
    yield None
    fine = fine[:, :DK_C] + fine[:, DK_C:]
    p, t = fine[GLA_FINE * c:(GLA_FINE + 1) * c], fine[(GLA_FINE + 1) * c:(GLA_FINE + 2) * c]
    k_bf = k.astype(BF16)
    a = pair_ref[rev, 0] * _dot_nt(q.astype(BF16), k_bf)
    for l in range(1, levels + 1):
        s = 1 << (l - 1)
        if l <= GLA_FINE:
            e = jnp.exp2(fine[(l - 1) * c:l * c])
            a = a + pair_ref[rev, l] * _dot_nt((q * e).astype(BF16), (k * e).astype(BF16))
        else:
            nb = c // (2 * s)
            split = lambda arr: arr.reshape(nb, 2, s, arr.shape[-1])
            join = lambda f, sec: jnp.stack([f, sec] if first_i == 0 else [sec, f], axis=1)
            p4, t4 = split(p), split(t)
            p_f, p_s, t_f, t_s = p4[:, first_i], p4[:, second_i], t4[:, first_i], t4[:, second_i]
            q_s = split(q)[:, second_i] * jnp.exp2(p_s)
            k_f = split(k)[:, first_i] * jnp.exp2(t_f - p_f)
            k_all = join(k_f.astype(BF16), split(k_bf)[:, second_i]).reshape(c, DK_C)
            r = _dot_nt(q_s.reshape(c // 2, DK_C).astype(BF16), k_all).reshape(nb, s, c)
            a4 = a.reshape(nb, 2, s, c)
            mask = pair_ref[rev, l].reshape(nb, 2, s, c)[:, second_i]
            a = join(a4[:, first_i], a4[:, second_i] + mask * r).reshape(c, c)
            t_new = t_f + t_s
            p = join(p_f, p_s + t_f).reshape(c, DK_C)
            t = join(t_new, t_new).reshape(c, DK_C)
        yield None
    s_old = s_ref[...]
    o_inter = _dot((q * jnp.exp2(p)).astype(BF16), s_old.astype(BF16))
    kd_t = (k * jnp.exp2(t - p)).T.astype(BF16)
    both = _dot(jnp.concatenate([a.astype(BF16), kd_t], axis=0), v)
    dec_col = jnp.exp2(t[0:8]).T[:, 0:1]
    s_ref[...] = s_old * dec_col + both[c:]
    yield both[:c] + o_inter


def _gla_kernel(*refs, zero_init):
    (qf_ref, kf_ref, vf_ref, zf_ref, qb_ref, kb_ref, vb_ref, zb_ref,
     wbf_ref, baf_ref, wbb_ref, bab_ref, pair_ref, fine_ref) = refs[:14]
    rest = refs[14:]
    if not zero_init:
        sfi_ref, sbi_ref = rest[:2]
        rest = rest[2:]
    of_ref, ob_ref, sfo_ref, sbo_ref, sf_ref, sb_ref = rest
    ci = pl.program_id(1)

    @pl.when(ci == 0)
    def _():
        if zero_init:
            sf_ref[...] = jnp.zeros_like(sf_ref)
            sb_ref[...] = jnp.zeros_like(sb_ref)
        else:
            sf_ref[...] = sfi_ref[...]
            sb_ref[...] = sbi_ref[...]

    scale = DK_C ** -0.5
    chains = []
    for h in range(H_C):
        ks, vs = slice(h * DK_C, (h + 1) * DK_C), slice(h * DV_C, (h + 1) * DV_C)
        chains.append((of_ref, vs, _gla_chain(
            qf_ref[:, ks].astype(F32) * scale, kf_ref[:, ks].astype(F32), vf_ref[:, vs],
            zf_ref[:, 0:GATE_RANK], wbf_ref[:, ks], baf_ref[:, ks], pair_ref, fine_ref, sf_ref.at[h], False)))
        chains.append((ob_ref, vs, _gla_chain(
            qb_ref[:, ks].astype(F32) * scale, kb_ref[:, ks].astype(F32), vb_ref[:, vs],
            zb_ref[:, GATE_RANK:2 * GATE_RANK], wbb_ref[:, ks], bab_ref[:, ks], pair_ref, fine_ref,
            sb_ref.at[h], True)))
    live = True
    while live:
        for o_ref, vs, chain in chains:
            out = next(chain, "done")
            if isinstance(out, str):
                live = False
            elif out is not None:
                o_ref[:, vs] = out.astype(o_ref.dtype)

    @pl.when(ci == pl.num_programs(1) - 1)
    def _():
        sfo_ref[...] = sf_ref[...]
        sbo_ref[...] = sb_ref[...]


def _gla(y, z, wb_f, ba_f, wb_b, ba_b, s_f, s_b, *, n_seq, s_len, row0, name):
    c = GLA_CHUNK
    n = s_len // c
    r0 = row0 // c
    zero_init = s_f is None
    pair, fine = _gla_masks(c)
    fwd = lambda b, i: b * n + i
    bwd = lambda b, i: b * n + (n - 1 - i)

    def dir_specs(rows):
        return [pl.BlockSpec((c, HK_C), lambda b, i: (r0 + rows(b, i), 0)),
                pl.BlockSpec((c, HK_C), lambda b, i: (r0 + rows(b, i), 1)),
                pl.BlockSpec((c, HV_C), lambda b, i: (r0 + rows(b, i), 1)),
                pl.BlockSpec((c, 128), lambda b, i: (r0 + rows(b, i), 0))]

    full = lambda arr: pl.BlockSpec(arr.shape, lambda b, i: (0,) * arr.ndim)
    state = pl.BlockSpec((None, H_C, DK_C, DV_C), lambda b, i: (b, 0, 0, 0))
    ba_f, ba_b = ba_f.reshape(1, -1), ba_b.reshape(1, -1)
    in_specs = (dir_specs(fwd) + dir_specs(bwd)
                + [full(wb_f), full(ba_f), full(wb_b), full(ba_b), full(pair), full(fine)])
    args = [y, y, y, z, y, y, y, z, wb_f, ba_f, wb_b, ba_b, pair, fine]
    if not zero_init:
        in_specs += [state, state]
        args += [s_f, s_b]
    return pl.pallas_call(
        functools.partial(_gla_kernel, zero_init=zero_init),
        grid=(n_seq, n),
        in_specs=in_specs,
        out_specs=[pl.BlockSpec((c, HV_C), lambda b, i: (fwd(b, i), 0)),
                   pl.BlockSpec((c, HV_C), lambda b, i: (bwd(b, i), 0)),
                   state, state],
        out_shape=[jax.ShapeDtypeStruct((n_seq * s_len, HV_C), BF16),
                   jax.ShapeDtypeStruct((n_seq * s_len, HV_C), BF16),
                   jax.ShapeDtypeStruct((n_seq, H_C, DK_C, DV_C), F32),
                   jax.ShapeDtypeStruct((n_seq, H_C, DK_C, DV_C), F32)],
        scratch_shapes=[pltpu.VMEM((H_C, DK_C, DV_C), F32), pltpu.VMEM((H_C, DK_C, DV_C), F32)],
        compiler_params=_params("arbitrary", "arbitrary"),
        name=name,
    )(*args)


def _gla_out_kernel(ofp_ref, ofs_ref, obp_ref, obs_ref, r_ref, go_ref, w_ref, x_ref, gpost_ref, gate_ref,
                    gn_ref, sh_ref, sc_ref, xo_ref, ho_ref, a_ref):
    o_all = _row_load([ofp_ref, ofs_ref]).astype(F32) + _row_load([obp_ref, obs_ref]).astype(F32)
    for h in range(H_C):
        vs = slice(h * DV_C, (h + 1) * DV_C)
        r = r_ref[:, vs].astype(F32)
        a = _rms(o_all[:, vs], go_ref[...]) * (r * (1.0 / (1.0 + jnp.exp(-r))))
        a_ref[:, vs] = a.astype(BF16)
    _residual_epilogue(_dot(a_ref[...], w_ref[...]), x_ref[...], gpost_ref, gate_ref,
                       (gn_ref, sh_ref, sc_ref), xo_ref, ho_ref)


def _gla_out(o_f, o_b, y, g_out, w, w_layer, x, mod, layer, g_post, nxt):
    tm = 256
    r_blk = (2 * HK_C + HV_C) // HV_C
    specs, args, out_specs, out_shape = _res_specs(tm, 0, M_ROWS, mod, layer, 2, g_post, nxt)
    of_specs, of_args = _row_specs(o_f, tm)
    ob_specs, ob_args = _row_specs(o_b, tm)
    return pl.pallas_call(
        _gla_out_kernel,
        grid=(M_ROWS // tm,),
        in_specs=of_specs + ob_specs + [
            pl.BlockSpec((tm, HV_C), lambda i: (i, r_blk)),
            pl.BlockSpec((1, DV_C), lambda i: (0, 0)),
            _resident_weight_spec(HV_C, w_layer),
            pl.BlockSpec((tm, D), lambda i: (i, 0))] + specs,
        out_specs=out_specs,
        out_shape=out_shape,
        scratch_shapes=[pltpu.VMEM((tm, HV_C), BF16)],
        compiler_params=_params("arbitrary"),
        name="gla_out",
    )(*of_args, *ob_args, y, g_out.reshape(1, -1), w, x, *args)


def _with_cache(cache, new):
    w = new.shape[-1]
    cat = jnp.concatenate([cache.reshape(N_SAMPLE_SEQ, PAST, w).astype(BF16),
                           new.reshape(N_SAMPLE_SEQ, SAMPLE_LEN, w).astype(BF16)], axis=1)
    return cat.reshape(N_SAMPLE_SEQ * (PAST + SAMPLE_LEN), w)


def kernel(x_prompt, x_sample, cache_mla_ckv, cache_mla_krope, cache_gqa_k, cache_gqa_v, state_gla_fwd, state_gla_bwd, c, c_ctx, w_mod, b_mod, g_pre_mix, g_post_mix, g_pre_ffn, g_post_ffn, w_in_ab, g_mla_q, g_mla_kv, w_mla_uq, w_mla_ukv, g_gqa_q, g_gqa_k, w_out_ab, w_in_c, w_gate_fwd_a, w_gate_fwd_b, b_gate_fwd, w_gate_bwd_a, w_gate_bwd_b, b_gate_bwd, g_gla_out, w_out_c, w_ffn_up, ffn_conv_w, ffn_conv_b, w_ffn_down):
    depth = w_mod.shape[0]
    cond = jnp.concatenate([c_ctx[None], c, jnp.zeros((N_MOD_ROWS - 1 - N_SAMPLE_SEQ, D), F32)], axis=0)
    mod = _modulation(cond, w_mod, b_mod).reshape(depth * N_MOD_ROWS * 6, 1, D)
    w_out_ab_bf, w_out_c_bf, w_down_bf = w_out_ab.astype(BF16), w_out_c.astype(BF16), w_ffn_down.astype(BF16)
    t_sample = PAST + SAMPLE_LEN

    x = (x_prompt.reshape(NP, D), x_sample.reshape(NS, D))
    h = _prenorm(x, g_pre_mix[0], mod, 0)
    caches, states = [], []
    for i in range(depth):
        j = i // 2
        ffn_pre = (g_pre_ffn[i], i, 3, 4)
        if i % 2 == 0:
            w = w_in_ab[j]
            w_in = jnp.concatenate(
                [w[:, 0:768], w[:, 832:2368], w[:, 768:832], jnp.zeros((D, IN_AB_PAD - 2368), F32)], axis=1)
            wq = w_mla_uq[j].reshape(Q_LORA, H_A, NOPE_A + ROPE_A)
            w_uq = jnp.concatenate([wq, jnp.zeros((Q_LORA, H_A, 256 - NOPE_A - ROPE_A), F32)],
                                   axis=2).reshape(Q_LORA, H_A * 256)
            y = _matmul(h, w_in, tm=1024, tn=IN_AB_PAD // 2, out_dtype=F32, name="in_ab")
            qa, qb, ckv, kr, kb, vb = _ab_post(y, g_mla_q[j], g_mla_kv[j], g_gqa_q[j], g_gqa_k[j], w_uq)
            ckv_all = jnp.concatenate([_with_cache(cache_mla_ckv[:, j], ckv[NP:]), ckv[:NP].astype(BF16)], axis=0)
            kv = _matmul(ckv_all, w_mla_ukv[j], tm=1024, tn=1024, out_dtype=BF16, name="kv_up")
            o_p = _attention(qa, qb, kv, kr[:NP].astype(BF16), kb[:NP].astype(BF16), vb[:NP].astype(BF16),
                             n_seq=N_PROMPT_SEQ, s_len=PROMPT_LEN, t_len=PROMPT_LEN, row0=0,
                             kv_row0=N_SAMPLE_SEQ * t_sample, name="attn_prompt")
            kr_cache = jnp.pad(cache_mla_krope[:, j], ((0, 0), (0, 0), (0, 128 - ROPE_A)))
            o_s = _attention(qa, qb, kv, _with_cache(kr_cache, kr[NP:]),
                             _with_cache(cache_gqa_k[:, j], kb[NP:]), _with_cache(cache_gqa_v[:, j], vb[NP:]),
                             n_seq=N_SAMPLE_SEQ, s_len=SAMPLE_LEN, t_len=t_sample, row0=NP, kv_row0=0,
                             name="attn_sample")
            x, h = _proj_res((o_p, o_s), w_out_ab_bf, j, x, mod, i, 2, g_post_mix[i], ffn_pre, tm=512,
                             name="out_ab")
            caches.append((ckv[:NP].reshape(N_PROMPT_SEQ, PROMPT_LEN, KV_LORA),
                           kr[:NP, :ROPE_A].reshape(N_PROMPT_SEQ, PROMPT_LEN, ROPE_A),
                           kb[:NP].reshape(N_PROMPT_SEQ, PROMPT_LEN, KV_B, HD_B),
                           vb[:NP].reshape(N_PROMPT_SEQ, PROMPT_LEN, KV_B, HD_B)))
        else:
            y = _matmul(h, w_in_c[j], tm=1024, tn=768, out_dtype=BF16, name="in_c")
            w_z = jnp.concatenate([w_gate_fwd_a[j], w_gate_bwd_a[j], jnp.zeros((D, 128 - 2 * GATE_RANK), F32)], axis=1)
            z = _matmul(h, w_z, tm=1024, tn=128, out_dtype=F32, name="gate_lowrank")
            gate_args = (w_gate_fwd_b[j], b_gate_fwd[j], w_gate_bwd_b[j], b_gate_bwd[j])
            of_p, ob_p, sf, sb = _gla(y, z, *gate_args, None, None,
                                      n_seq=N_PROMPT_SEQ, s_len=PROMPT_LEN, row0=0, name="gla_prompt")
            of_s, ob_s, _, _ = _gla(y, z, *gate_args, state_gla_fwd[:, j], state_gla_bwd[:, j],
                                    n_seq=N_SAMPLE_SEQ, s_len=SAMPLE_LEN, row0=NP, name="gla_sample")
            x, h = _gla_out((of_p, of_s), (ob_p, ob_s), y, g_gla_out[j], w_out_c_bf, j, x, mod, i,
                            g_post_mix[i], ffn_pre)
            states.append((sf, sb))
        act = _ffn_up(h, w_ffn_up, ffn_conv_w, ffn_conv_b, i)
        if i + 1 < depth:
            nxt = (g_pre_mix[i + 1], i + 1, 0, 1)
            x, h = _proj_res(act, w_down_bf, i, x, mod, i, 5, g_post_ffn[i], nxt, tm=256, name="ffn_down")
        else:
            y_prompt, _ = _proj_res(act, w_down_bf, i, x, mod, i, 5, g_post_ffn[i], None, tm=256,
                                    name="ffn_down_prompt", row0=0, n_rows=NP)
            y_sample, _ = _proj_res(act, w_down_bf, i, x, mod, i, 5, g_post_ffn[i], None, tm=256,
                                    name="ffn_down_sample", row0=NP, n_rows=NS)

    new_ckv, new_krope, new_k, new_v = (jnp.stack(t, axis=1) for t in zip(*caches))
    new_sf, new_sb = (jnp.stack(t, axis=1) for t in zip(*states))
    return (y_prompt.reshape(N_PROMPT_SEQ, PROMPT_LEN, D), y_sample.reshape(N_SAMPLE_SEQ, SAMPLE_LEN, D),
            new_ckv, new_krope, new_k, new_v, new_sf, new_sb)
```

```python
import functools
import math

import numpy as np
import jax
import jax.numpy as jnp
from jax import lax
from jax.experimental import pallas as pl
from jax.experimental.pallas import tpu as pltpu

D = 2048
N_PROMPT_SEQ, PROMPT_LEN = 16, 256
N_SAMPLE_SEQ, SAMPLE_LEN = 4, 2048
PAST = 512
GRID_W = 64
NP = N_PROMPT_SEQ * PROMPT_LEN
NS = N_SAMPLE_SEQ * SAMPLE_LEN
M_ROWS = NP + NS
EPS = 1e-6
ROPE_THETA = 10000.0

H_A, NOPE_A, ROPE_A, V_A = 8, 128, 64, 128
Q_LORA, KV_LORA = 512, 256
H_B, KV_B, HD_B = 8, 2, 128
IN_AB_PAD = 2560
H_C, DK_C, DV_C = 4, 256, 512
HK_C, HV_C = H_C * DK_C, H_C * DV_C
GATE_RANK = 16
GATE_TAU = 16.0
D_FF = 5632
GLA_CHUNK = 128
GLA_FINE = 3
N_MOD_ROWS = 8

VMEM_LIMIT_BYTES = 56 * 1024 * 1024
BF16 = jnp.bfloat16
F32 = jnp.float32


def _params(*sem):
    return pltpu.CompilerParams(dimension_semantics=sem, vmem_limit_bytes=VMEM_LIMIT_BYTES)


def _mod_row(i, tm):
    n_prompt_tiles = NP // tm
    per_seq = SAMPLE_LEN // tm
    return jnp.where(i < n_prompt_tiles, 0, 1 + (i - n_prompt_tiles) // per_seq)


def _mod_spec(layer, piece, tm, tile0=0):
    def index_map(i, *_):
        return ((layer * N_MOD_ROWS + _mod_row(tile0 + i, tm)) * 6 + piece, 0, 0)
    return pl.BlockSpec((None, 1, D), index_map)


def _rms(x, g):
    return x * lax.rsqrt(jnp.mean(x * x, axis=-1, keepdims=True) + EPS) * g


def _dot(a, b):
    return jnp.dot(a, b, preferred_element_type=F32)


def _dot_nt(a, b):
    return lax.dot_general(a, b, (((1,), (1,)), ((), ())), preferred_element_type=F32)


def _split2(x):
    hi = x.astype(BF16)
    lo = (x - hi.astype(F32)).astype(BF16)
    return hi, lo


def _mod_kernel(c_ref, w_ref, b_ref, o_ref):
    c = c_ref[...]
    a = c * (1.0 / (1.0 + jnp.exp(-c)))
    a_hi, a_lo = _split2(a)
    w_hi, w_lo = _split2(w_ref[...])
    r = _dot(jnp.concatenate([a_hi, a_lo], axis=0), w_hi)
    o_ref[...] = r[:N_MOD_ROWS] + r[N_MOD_ROWS:] + _dot(a_hi, w_lo) + b_ref[...]


def _modulation(cond, w_mod, b_mod):
    depth = w_mod.shape[0]
    n = w_mod.shape[2]
    tn = 1024
    return pl.pallas_call(
        _mod_kernel,
        grid=(depth, n // tn),
        in_specs=[pl.BlockSpec((N_MOD_ROWS, D), lambda l, j: (0, 0)),
                  pl.BlockSpec((None, D, tn), lambda l, j: (l, 0, j)),
                  pl.BlockSpec((None, 1, tn), lambda l, j: (l, 0, j))],
        out_specs=pl.BlockSpec((None, N_MOD_ROWS, tn), lambda l, j: (l, 0, j)),
        out_shape=jax.ShapeDtypeStruct((depth, N_MOD_ROWS, n), F32),
        compiler_params=_params("arbitrary", "arbitrary"),
        name="modulation",
    )(cond, w_mod, b_mod.reshape(depth, 1, n))


def _prenorm_kernel(xp_ref, xs_ref, g_ref, sh_ref, sc_ref, h_ref):
    x = _row_load([xp_ref, xs_ref])
    h_ref[...] = (_rms(x, g_ref[...]) * (1.0 + sc_ref[...]) + sh_ref[...]).astype(h_ref.dtype)


def _prenorm(x_pair, g, mod, layer):
    tm = 512
    x_specs, x_args = _row_specs(x_pair, tm)
    return pl.pallas_call(
        _prenorm_kernel,
        grid=(M_ROWS // tm,),
        in_specs=x_specs + [pl.BlockSpec((1, D), lambda i: (0, 0)),
                            _mod_spec(layer, 0, tm), _mod_spec(layer, 1, tm)],
        out_specs=pl.BlockSpec((tm, D), lambda i: (i, 0)),
        out_shape=jax.ShapeDtypeStruct((M_ROWS, D), BF16),
        compiler_params=_params("arbitrary"),
        name="prenorm",
    )(*x_args, g.reshape(1, D), mod, mod)


def _matmul_kernel(x_ref, w_ref, o_ref):
    o_ref[...] = _dot(x_ref[...].astype(BF16), w_ref[...].astype(BF16)).astype(o_ref.dtype)


def _matmul(x, w, *, tm, tn, out_dtype, name):
    m, k = x.shape
    n = w.shape[1]
    return pl.pallas_call(
        _matmul_kernel,
        grid=(n // tn, m // tm),
        in_specs=[pl.BlockSpec((tm, k), lambda j, i: (i, 0)),
                  pl.BlockSpec((k, tn), lambda j, i: (0, j))],
        out_specs=pl.BlockSpec((tm, tn), lambda j, i: (i, j)),
        out_shape=jax.ShapeDtypeStruct((m, n), out_dtype),
        compiler_params=_params("arbitrary", "arbitrary"),
        name=name,
    )(x, w)


def _matmul_t_kernel(wt_ref, x_ref, o_ref):
    o_ref[...] = _dot_nt(wt_ref[...].astype(BF16), x_ref[...].astype(BF16)).astype(o_ref.dtype)


def _matmul_t(w_t, x, *, tm, out_dtype, name):
    n, k = w_t.shape
    m = x.shape[0]
    return pl.pallas_call(
        _matmul_t_kernel,
        grid=(m // tm,),
        in_specs=[pl.BlockSpec((n, k), lambda i: (0, 0)),
                  pl.BlockSpec((tm, k), lambda i: (i, 0))],
        out_specs=pl.BlockSpec((n, tm), lambda i: (0, i)),
        out_shape=jax.ShapeDtypeStruct((n, m), out_dtype),
        compiler_params=_params("arbitrary"),
        name=name,
    )(w_t, x)


def _rope_tables(rot_dim, tm):
    q = rot_dim // 4
    inv = ROPE_THETA ** (-np.arange(q, dtype=np.float64) / q)
    tok = np.arange(SAMPLE_LEN)
    ang_r = (tok // GRID_W)[:, None] * inv
    ang_c = (tok % GRID_W)[:, None] * inv
    ang = np.concatenate([ang_r, ang_r, ang_c, ang_c], axis=-1)
    cos, sin = np.cos(ang), np.sin(ang)
    first = (np.arange(rot_dim) % (2 * q)) < q
    sin_m = np.where(first, -sin, 0.0)
    sin_p = np.where(first, 0.0, sin)

    def widen(t, ident):
        full = np.full((tm + SAMPLE_LEN, 128), ident, np.float64)
        full[tm:, :rot_dim] = t
        return jnp.asarray(full, F32)

    return widen(cos, 1.0), widen(sin_m, 0.0), widen(sin_p, 0.0), q


def _rope128(x, cos, sin_m, sin_p, q):
    return (x * cos + pltpu.roll(x, 128 - q, 1) * sin_m + pltpu.roll(x, q, 1) * sin_p)


def _ab_post_kernel(y_ref, gq_ref, gkv_ref, gqn_ref, gkn_ref, wuq_ref,
                    ca_ref, ma_ref, pa_ref, cb_ref, mb_ref, pb_ref,
                    qa_ref, qb_ref, ckv_ref, kr_ref, kb_ref, vb_ref, *, qa_rot, qb_rot):
    ca, ma, pa = ca_ref[...], ma_ref[...], pa_ref[...]
    cb, mb, pb = cb_ref[...], mb_ref[...], pb_ref[...]
    scale_a = (NOPE_A + ROPE_A) ** -0.5 * math.log2(math.e)
    scale_b = HD_B ** -0.5 * math.log2(math.e)

    cq = _rms(y_ref[:, 0:Q_LORA], gq_ref[...])
    qa = _dot(cq.astype(BF16), wuq_ref[...].astype(BF16)) * scale_a
    for h in range(H_A):
        base = h * 256
        qa_ref[:, base:base + 128] = qa[:, base:base + 128].astype(qa_ref.dtype)
        qa_ref[:, base + 128:base + 256] = _rope128(
            qa[:, base + 128:base + 256], ca, ma, pa, qa_rot).astype(qa_ref.dtype)

    ckv_ref[...] = _rms(y_ref[:, 512:768], gkv_ref[...])
    kr_ref[...] = _rope128(y_ref[:, 2304:2432], ca, ma, pa, qa_rot)

    for h in range(H_B):
        xb = _rms(y_ref[:, 768 + h * 128:768 + (h + 1) * 128], gqn_ref[...])
        qb_ref[:, h * 128:(h + 1) * 128] = (
            _rope128(xb, cb, mb, pb, qb_rot) * scale_b).astype(qb_ref.dtype)
    for g in range(KV_B):
        xk = _rms(y_ref[:, 1792 + g * 128:1792 + (g + 1) * 128], gkn_ref[...])
        kb_ref[:, g * 128:(g + 1) * 128] = _rope128(xk, cb, mb, pb, qb_rot)
    vb_ref[...] = y_ref[:, 2048:2304]


def _ab_post(y, g_q, g_kv, g_qn, g_kn, w_uq_pad):
    tm = 256
    ca, ma, pa, qa_rot = _rope_tables(ROPE_A, tm)
    cb, mb, pb, qb_rot = _rope_tables(HD_B, tm)
    n_prompt_tiles = NP // tm
    per_seq = SAMPLE_LEN // tm

    def tab_map(i):
        return (jnp.where(i < n_prompt_tiles, 0, 1 + (i - n_prompt_tiles) % per_seq), 0)

    tab = pl.BlockSpec((tm, 128), tab_map)
    row = lambda w: pl.BlockSpec((tm, w), lambda i: (i, 0))
    vec = lambda w: pl.BlockSpec((1, w), lambda i: (0, 0))
    return pl.pallas_call(
        functools.partial(_ab_post_kernel, qa_rot=qa_rot, qb_rot=qb_rot),
        grid=(M_ROWS // tm,),
        in_specs=[row(IN_AB_PAD), vec(Q_LORA), vec(KV_LORA), vec(HD_B), vec(HD_B),
                  pl.BlockSpec((Q_LORA, H_A * 256), lambda i: (0, 0)),
                  tab, tab, tab, tab, tab, tab],
        out_specs=[row(H_A * 256), row(H_B * HD_B), row(KV_LORA), row(128),
                   row(KV_B * HD_B), row(KV_B * HD_B)],
        out_shape=[jax.ShapeDtypeStruct((M_ROWS, H_A * 256), BF16),
                   jax.ShapeDtypeStruct((M_ROWS, H_B * HD_B), BF16),
                   jax.ShapeDtypeStruct((M_ROWS, KV_LORA), F32),
                   jax.ShapeDtypeStruct((M_ROWS, 128), F32),
                   jax.ShapeDtypeStruct((M_ROWS, KV_B * HD_B), F32),
                   jax.ShapeDtypeStruct((M_ROWS, KV_B * HD_B), F32)],
        compiler_params=_params("arbitrary"),
        name="ab_post",
    )(y, g_q.reshape(1, -1), g_kv.reshape(1, -1), g_qn.reshape(1, -1), g_kn.reshape(1, -1),
      w_uq_pad, ca, ma, pa, cb, mb, pb)


ATTN_SCORES_AHEAD = 3


def _attn_kernel(qa_ref, qb_ref, kn_ref, kr_ref, kb_ref, vat_ref, vbt_ref, o_ref):
    kr = kr_ref[...]
    rep = H_B // KV_B
    n_heads = H_A + H_B

    def scores_t(h):
        if h < H_A:
            k = jnp.concatenate([kn_ref[:, h * 128:(h + 1) * 128], kr], axis=1)
            return _dot_nt(k, qa_ref[:, h * 256:(h + 1) * 256])
        g = (h - H_A) // rep
        return _dot_nt(kb_ref[:, g * 128:(g + 1) * 128], qb_ref[:, (h - H_A) * 128:(h - H_A + 1) * 128])

    def values_t(h):
        if h < H_A:
            return vat_ref[h * 128:(h + 1) * 128, :]
        g = (h - H_A) // rep
        return vbt_ref[g * 128:(g + 1) * 128, :]

    queue = [scores_t(h) for h in range(ATTN_SCORES_AHEAD)]
    for h in range(n_heads):
        s_t = queue.pop(0)
        if h + ATTN_SCORES_AHEAD < n_heads:
            queue.append(scores_t(h + ATTN_SCORES_AHEAD))
        m = jnp.max(s_t, axis=0, keepdims=True)
        e = jnp.exp2(s_t - m)
        l = jnp.sum(e, axis=0, keepdims=True)
        o_t = _dot(values_t(h), e.astype(BF16)) / l
        o_ref[:, h * 128:(h + 1) * 128] = o_t.T.astype(o_ref.dtype)


def _attention(qa, qb, kn, kr, kb, vat, vbt, *, n_seq, s_len, t_len, row0, kv_row0, name):
    tq = 256
    nq = s_len // tq
    q0 = row0 // tq
    kv0 = kv_row0 // t_len
    qspec = lambda w: pl.BlockSpec((tq, w), lambda b, i: (q0 + b * nq + i, 0))
    kspec = lambda w, b0=0: pl.BlockSpec((t_len, w), lambda b, i: (b0 + b, 0), pipeline_mode=pl.Buffered(1))
    vspec = lambda r, b0=0: pl.BlockSpec((r, t_len), lambda b, i: (0, b0 + b), pipeline_mode=pl.Buffered(1))
    return pl.pallas_call(
        _attn_kernel,
        grid=(n_seq, nq),
        in_specs=[qspec(H_A * 256), qspec(H_B * HD_B), kspec(H_A * NOPE_A, kv0), kspec(128),
                  kspec(KV_B * HD_B), vspec(H_A * V_A, kv0), vspec(KV_B * HD_B)],
        out_specs=pl.BlockSpec((tq, D), lambda b, i: (b * nq + i, 0)),
        out_shape=jax.ShapeDtypeStruct((n_seq * s_len, D), BF16),
        compiler_params=_params("arbitrary", "arbitrary"),
        name=name,
    )(qa, qb, kn, kr, kb, vat, vbt)


def _row_specs(arr, tm, tile0=0):
    if not isinstance(arr, tuple):
        return [pl.BlockSpec((tm, arr.shape[1]), lambda i, *_: (tile0 + i, 0))], [arr]
    npt = NP // tm
    w = arr[0].shape[1]
    return ([pl.BlockSpec((tm, w), lambda i, *_: (jnp.minimum(tile0 + i, npt - 1), 0)),
             pl.BlockSpec((tm, w), lambda i, *_: (jnp.maximum(tile0 + i - npt, 0), 0))], list(arr))


def _row_load(refs, tile0=0):
    if len(refs) == 1:
        return refs[0][...]
    tm = refs[0].shape[0]
    return jnp.where(tile0 + pl.program_id(0) < NP // tm, refs[0][...], refs[1][...])


def _residual_epilogue(acc, x, gpost_ref, gate_ref, nxt, xo_ref, ho_ref):
    x_new = x + gate_ref[...] * _rms(acc, gpost_ref[...])
    xo_ref[...] = x_new
    if nxt is not None:
        gn_ref, sh_ref, sc_ref = nxt
        ho_ref[...] = (_rms(x_new, gn_ref[...]) * (1.0 + sc_ref[...]) + sh_ref[...]).astype(ho_ref.dtype)


def _proj_res_kernel(*refs, n_a, n_x, emit_h, tile0):
    a_refs, w_ref, x_refs, rest = refs[:n_a], refs[n_a], refs[n_a + 1:n_a + 1 + n_x], refs[n_a + 1 + n_x:]
    if emit_h:
        gpost_ref, gate_ref, gn_ref, sh_ref, sc_ref, xo_ref, ho_ref = rest
        nxt = (gn_ref, sh_ref, sc_ref)
    else:
        gpost_ref, gate_ref, xo_ref = rest
        nxt, ho_ref = None, None
    acc = _dot(_row_load(a_refs, tile0), w_ref[...])
    _residual_epilogue(acc, _row_load(x_refs, tile0), gpost_ref, gate_ref, nxt, xo_ref, ho_ref)


def _res_specs(tm, tile0, n_rows, mod, layer, gate_piece, g_post, nxt):
    vec = pl.BlockSpec((1, D), lambda i, *_: (0, 0))
    specs = [vec, _mod_spec(layer, gate_piece, tm, tile0)]
    args = [g_post.reshape(1, D), mod]
    if nxt is not None:
        g_next, layer_next, sh_piece, sc_piece = nxt
        specs += [vec, _mod_spec(layer_next, sh_piece, tm, tile0), _mod_spec(layer_next, sc_piece, tm, tile0)]
        args += [g_next.reshape(1, D), mod, mod]
    out_specs = [pl.BlockSpec((tm, D), lambda i, *_: (i, 0))]
    out_shape = [jax.ShapeDtypeStruct((n_rows, D), F32)]
    if nxt is not None:
        out_specs.append(pl.BlockSpec((tm, D), lambda i, *_: (i, 0)))
        out_shape.append(jax.ShapeDtypeStruct((n_rows, D), BF16))
    return specs, args, out_specs, out_shape


def _resident_weight_spec(kdim, w_layer):
    return pl.BlockSpec((None, kdim, D), lambda i: (w_layer, 0, 0), pipeline_mode=pl.Buffered(1))


def _proj_res(a, w, w_layer, x, mod, layer, gate_piece, g_post, nxt, *, tm, name, row0=0, n_rows=M_ROWS):
    tile0 = row0 // tm
    a_specs, a_args = _row_specs(a, tm, tile0)
    x_specs, x_args = _row_specs(x, tm, tile0)
    kdim = a_args[0].shape[1]
    specs, args, out_specs, out_shape = _res_specs(tm, tile0, n_rows, mod, layer, gate_piece, g_post, nxt)
    out = pl.pallas_call(
        functools.partial(_proj_res_kernel, n_a=len(a_args), n_x=len(x_args), emit_h=nxt is not None, tile0=tile0),
        grid=(n_rows // tm,),
        in_specs=a_specs + [_resident_weight_spec(kdim, w_layer)] + x_specs + specs,
        out_specs=out_specs,
        out_shape=out_shape,
        compiler_params=_params("arbitrary"),
        name=name,
    )(*a_args, w, *x_args, *args)
    return out if nxt is not None else (out[0], None)


def _ffn_up_kernel(h_ref, hp_ref, hn_ref, wa_ref, wg_ref, cwa_ref, cwg_ref, cba_ref, cbg_ref,
                   mp_ref, mn_ref, o_ref, wbf_ref):
    tm = h_ref.shape[0]
    halo = hp_ref.shape[0]

    @pl.when(pl.program_id(1) == 0)
    def _():
        wbf_ref[0] = wa_ref[...].astype(BF16)
        wbf_ref[1] = wg_ref[...].astype(BF16)

    h = h_ref[...]
    edge = jnp.concatenate([hp_ref[...], hn_ref[...]], axis=0)
    row8 = lax.broadcasted_iota(jnp.int32, (8, 1), 0)
    n_seg = tm // PROMPT_LEN

    def patch_edges(shifted, tile_edge_row, at_start, mask_ref):
        pieces = []
        for seg in range(n_seg):
            r = seg * PROMPT_LEN if at_start else (seg + 1) * PROMPT_LEN - 1
            g0 = r - r % 8
            grp = shifted[g0:g0 + 8]
            at_tile_edge = seg == 0 if at_start else seg == n_seg - 1
            src = tile_edge_row if at_tile_edge else grp
            grp = jnp.where(row8 == r % 8, src * mask_ref[r:r + 1, :], grp)
            if at_start:
                pieces += [grp, shifted[g0 + 8:(seg + 1) * PROMPT_LEN]]
            else:
                pieces += [shifted[seg * PROMPT_LEN:g0], grp]
        return jnp.concatenate(pieces, axis=0)

    def branch(which, cw_ref, cb_ref):
        w = wbf_ref[which]
        u = _dot(h, w)
        u_edge = _dot(edge, w)
        prev = patch_edges(pltpu.roll(u, 1, 0), u_edge[halo - 1:halo], True, mp_ref)
        nxt = patch_edges(pltpu.roll(u, tm - 1, 0), u_edge[halo:halo + 1], False, mn_ref)
        cw = cw_ref[...]
        return prev * cw[0:1] + u * cw[1:2] + nxt * cw[2:3] + cb_ref[...]

    a = branch(0, cwa_ref, cba_ref)
    g = branch(1, cwg_ref, cbg_ref)
    o_ref[...] = (g * (1.0 / (1.0 + jnp.exp(-g))) * a).astype(o_ref.dtype)


def _seq_edge_masks():
    pos = np.concatenate([np.tile(np.arange(PROMPT_LEN), N_PROMPT_SEQ),
                          np.tile(np.arange(SAMPLE_LEN), N_SAMPLE_SEQ)])
    lens = np.concatenate([np.full(NP, PROMPT_LEN), np.full(NS, SAMPLE_LEN)])
    has_prev = (pos > 0).astype(np.float32).reshape(M_ROWS, 1)
    has_next = (pos < lens - 1).astype(np.float32).reshape(M_ROWS, 1)
    return jnp.asarray(has_prev), jnp.asarray(has_next)


def _ffn_up(h, w_up, conv_w, conv_b, layer):
    tm, tn, halo = 1024, 512, 16
    nj = D_FF // tn
    r = tm // halo
    last = M_ROWS // halo - 1
    has_prev, has_next = _seq_edge_masks()
    conv_b = conv_b.reshape(conv_b.shape[0], 1, 2 * D_FF)
    col = lambda rows, half: pl.BlockSpec((None, rows, tn), lambda j, i: (layer, 0, j + half * nj))
    return pl.pallas_call(
        _ffn_up_kernel,
        grid=(nj, M_ROWS // tm),
        in_specs=[pl.BlockSpec((tm, D), lambda j, i: (i, 0)),
                  pl.BlockSpec((halo, D), lambda j, i: (jnp.maximum(i * r - 1, 0), 0)),
                  pl.BlockSpec((halo, D), lambda j, i: (jnp.minimum((i + 1) * r, last), 0)),
                  col(D, 0), col(D, 1), col(3, 0), col(3, 1), col(1, 0), col(1, 1),
                  pl.BlockSpec((tm, 1), lambda j, i: (i, 0)),
                  pl.BlockSpec((tm, 1), lambda j, i: (i, 0))],
        out_specs=pl.BlockSpec((tm, tn), lambda j, i: (i, j)),
        out_shape=jax.ShapeDtypeStruct((M_ROWS, D_FF), BF16),
        scratch_shapes=[pltpu.VMEM((2, D, tn), BF16)],
        compiler_params=_params("arbitrary", "arbitrary"),
        name="ffn_up",
    )(h, h, h, w_up, w_up, conv_w, conv_w, conv_b, conv_b, has_prev, has_next)


def _gla_masks(c):
    levels = int(math.log2(c))
    t = np.arange(c)
    pair = [t[:, None] == t[None, :]]
    for l in range(1, levels + 1):
        blk = t >> l
        up = ((t >> (l - 1)) & 1) == 1
        pair.append((blk[:, None] == blk[None, :]) & up[:, None] & ~up[None, :])
    pair = np.stack(pair).astype(np.float32)
    pair = np.stack([pair, pair.transpose(0, 2, 1)])
    mats = []
    for l in range(1, GLA_FINE + 1):
        split = ((t >> l) << l) + (1 << (l - 1)) - 1
        lo, hi = np.minimum(t, split), np.maximum(t, split)
        mats.append((t[None, :] > lo[:, None]) & (t[None, :] <= hi[:, None]))
    same = (t >> GLA_FINE)[:, None] == (t >> GLA_FINE)[None, :]
    mats += [same & (t[None, :] <= t[:, None]), same]
    fine = np.stack([np.concatenate(mats, axis=0),
                     np.concatenate([m[::-1, ::-1] for m in mats], axis=0)]).astype(np.float32)
    return jnp.asarray(pair), jnp.asarray(fine, BF16)


def _gla_chain(q, k, v, za, wb, ba, pair_ref, fine_ref, s_ref, reverse):
    c = q.shape[0]
    levels = pair_ref.shape[1] - 1
    za_hi, za_lo = _split2(za)
    wb_hi, wb_lo = _split2(wb)
    z = _dot(za_hi, wb_hi) + _dot(za_lo, wb_hi) + _dot(za_hi, wb_lo) + ba
    x = (jnp.minimum(z, 0.0) - jnp.log(1.0 + jnp.exp(-jnp.abs(z)))) * (math.log2(math.e) / GATE_TAU)
    yield None

    rev = 1 if reverse else 0
    first_i, second_i = (1, 0) if reverse else (0, 1)
    x_hi, x_lo = _split2(x)
    fine = _dot(fine_ref[rev], jnp.concatenate([x_hi, x_lo], axis=1))
    yield None
    fine = fine[:, :DK_C] + fine[:, DK_C:]
    p, t = fine[GLA_FINE * c:(GLA_FINE + 1) * c], fine[(GLA_FINE + 1) * c:(GLA_FINE + 2) * c]
    k_bf = k.astype(BF16)
    a = pair_ref[rev, 0] * _dot_nt(q.astype(BF16), k_bf)
    for l in range(1, levels + 1):
        s = 1 << (l - 1)
        if l <= GLA_FINE:
            e = jnp.exp2(fine[(l - 1) * c:l * c])
            a = a + pair_ref[rev, l] * _dot_nt((q * e).astype(BF16), (k * e).astype(BF16))
        else:
            nb = c // (2 * s)
            split = lambda arr: arr.reshape(nb, 2, s, arr.shape[-1])
            join = lambda f, sec: jnp.stack([f, sec] if first_i == 0 else [sec, f], axis=1)
            p4, t4 = split(p), split(t)
            p_f, p_s, t_f, t_s = p4[:, first_i], p4[:, second_i], t4[:, first_i], t4[:, second_i]
            q_s = split(q)[:, second_i] * jnp.exp2(p_s)
            k_f = split(k)[:, first_i] * jnp.exp2(t_f - p_f)
            k_all = join(k_f.astype(BF16), split(k_bf)[:, second_i]).reshape(c, DK_C)
            r = _dot_nt(q_s.reshape(c // 2, DK_C).astype(BF16), k_all).reshape(nb, s, c)
            a4 = a.reshape(nb, 2, s, c)
            mask = pair_ref[rev, l].reshape(nb, 2, s, c)[:, second_i]
            a = join(a4[:, first_i], a4[:, second_i] + mask * r).reshape(c, c)
            t_new = t_f + t_s
            p = join(p_f, p_s + t_f).reshape(c, DK_C)
            t = join(t_new, t_new).reshape(c, DK_C)
        yield None
    s_old = s_ref[...]
    o_inter = _dot((q * jnp.exp2(p)).astype(BF16), s_old.astype(BF16))
    kd_t = (k * jnp.exp2(t - p)).T.astype(BF16)
    both = _dot(jnp.concatenate([a.astype(BF16), kd_t], axis=0), v)
    dec_col = jnp.exp2(t[0:8]).T[:, 0:1]
    s_ref[...] = s_old * dec_col + both[c:]
    yield both[:c] + o_inter


def _gla_kernel(*refs, zero_init):
    (qf_ref, kf_ref, vf_ref, zf_ref, qb_ref, kb_ref, vb_ref, zb_ref,
     wbf_ref, baf_ref, wbb_ref, bab_ref, pair_ref, fine_ref) = refs[:14]
    rest = refs[14:]
    if not zero_init:
        sfi_ref, sbi_ref = rest[:2]
        rest = rest[2:]
    of_ref, ob_ref, sfo_ref, sbo_ref, sf_ref, sb_ref = rest
    ci = pl.program_id(1)

    @pl.when(ci == 0)
    def _():
        if zero_init:
            sf_ref[...] = jnp.zeros_like(sf_ref)
            sb_ref[...] = jnp.zeros_like(sb_ref)
        else:
            sf_ref[...] = sfi_ref[...]
            sb_ref[...] = sbi_ref[...]

    scale = DK_C ** -0.5
    chains = []
    for h in range(H_C):
        ks, vs = slice(h * DK_C, (h + 1) * DK_C), slice(h * DV_C, (h + 1) * DV_C)
        chains.append((of_ref, vs, _gla_chain(
            qf_ref[:, ks].astype(F32) * scale, kf_ref[:, ks].astype(F32), vf_ref[:, vs],
            zf_ref[:, 0:GATE_RANK], wbf_ref[:, ks], baf_ref[:, ks], pair_ref, fine_ref, sf_ref.at[h], False)))
        chains.append((ob_ref, vs, _gla_chain(
            qb_ref[:, ks].astype(F32) * scale, kb_ref[:, ks].astype(F32), vb_ref[:, vs],
            zb_ref[:, GATE_RANK:2 * GATE_RANK], wbb_ref[:, ks], bab_ref[:, ks], pair_ref, fine_ref,
            sb_ref.at[h], True)))
    live = True
    while live:
        for o_ref, vs, chain in chains:
            out = next(chain, "done")
            if isinstance(out, str):
                live = False
            elif out is not None:
                o_ref[:, vs] = out.astype(o_ref.dtype)

    @pl.when(ci == pl.num_programs(1) - 1)
    def _():
        sfo_ref[...] = sf_ref[...]
        sbo_ref[...] = sb_ref[...]


def _gla(y, z, wb_f, ba_f, wb_b, ba_b, s_f, s_b, *, n_seq, s_len, row0, name):
    c = GLA_CHUNK
    n = s_len // c
    r0 = row0 // c
    zero_init = s_f is None
    pair, fine = _gla_masks(c)
    fwd = lambda b, i: b * n + i
    bwd = lambda b, i: b * n + (n - 1 - i)

    def dir_specs(rows):
        return [pl.BlockSpec((c, HK_C), lambda b, i: (r0 + rows(b, i), 0)),
                pl.BlockSpec((c, HK_C), lambda b, i: (r0 + rows(b, i), 1)),
                pl.BlockSpec((c, HV_C), lambda b, i: (r0 + rows(b, i), 1)),
                pl.BlockSpec((c, 128), lambda b, i: (r0 + rows(b, i), 0))]

    full = lambda arr: pl.BlockSpec(arr.shape, lambda b, i: (0,) * arr.ndim)
    state = pl.BlockSpec((None, H_C, DK_C, DV_C), lambda b, i: (b, 0, 0, 0))
    ba_f, ba_b = ba_f.reshape(1, -1), ba_b.reshape(1, -1)
    in_specs = (dir_specs(fwd) + dir_specs(bwd)
                + [full(wb_f), full(ba_f), full(wb_b), full(ba_b), full(pair), full(fine)])
    args = [y, y, y, z, y, y, y, z, wb_f, ba_f, wb_b, ba_b, pair, fine]
    if not zero_init:
        in_specs += [state, state]
        args += [s_f, s_b]
    return pl.pallas_call(
        functools.partial(_gla_kernel, zero_init=zero_init),
        grid=(n_seq, n),
        in_specs=in_specs,
        out_specs=[pl.BlockSpec((c, HV_C), lambda b, i: (fwd(b, i), 0)),
                   pl.BlockSpec((c, HV_C), lambda b, i: (bwd(b, i), 0)),
                   state, state],
        out_shape=[jax.ShapeDtypeStruct((n_seq * s_len, HV_C), BF16),
                   jax.ShapeDtypeStruct((n_seq * s_len, HV_C), BF16),
                   jax.ShapeDtypeStruct((n_seq, H_C, DK_C, DV_C), F32),
                   jax.ShapeDtypeStruct((n_seq, H_C, DK_C, DV_C), F32)],
        scratch_shapes=[pltpu.VMEM((H_C, DK_C, DV_C), F32), pltpu.VMEM((H_C, DK_C, DV_C), F32)],
        compiler_params=_params("arbitrary", "arbitrary"),
        name=name,
    )(*args)


def _gla_out_kernel(ofp_ref, ofs_ref, obp_ref, obs_ref, r_ref, go_ref, w_ref, x_ref, gpost_ref, gate_ref,
                    gn_ref, sh_ref, sc_ref, xo_ref, ho_ref, a_ref):
    o_all = _row_load([ofp_ref, ofs_ref]).astype(F32) + _row_load([obp_ref, obs_ref]).astype(F32)
    for h in range(H_C):
        vs = slice(h * DV_C, (h + 1) * DV_C)
        r = r_ref[:, vs].astype(F32)
        a = _rms(o_all[:, vs], go_ref[...]) * (r * (1.0 / (1.0 + jnp.exp(-r))))
        a_ref[:, vs] = a.astype(BF16)
    _residual_epilogue(_dot(a_ref[...], w_ref[...]), x_ref[...], gpost_ref, gate_ref,
                       (gn_ref, sh_ref, sc_ref), xo_ref, ho_ref)


def _gla_out(o_f, o_b, y, g_out, w, w_layer, x, mod, layer, g_post, nxt):
    tm = 256
    r_blk = (2 * HK_C + HV_C) // HV_C
    specs, args, out_specs, out_shape = _res_specs(tm, 0, M_ROWS, mod, layer, 2, g_post, nxt)
    of_specs, of_args = _row_specs(o_f, tm)
    ob_specs, ob_args = _row_specs(o_b, tm)
    return pl.pallas_call(
        _gla_out_kernel,
        grid=(M_ROWS // tm,),
        in_specs=of_specs + ob_specs + [
            pl.BlockSpec((tm, HV_C), lambda i: (i, r_blk)),
            pl.BlockSpec((1, DV_C), lambda i: (0, 0)),
            _resident_weight_spec(HV_C, w_layer),
            pl.BlockSpec((tm, D), lambda i: (i, 0))] + specs,
        out_specs=out_specs,
        out_shape=out_shape,
        scratch_shapes=[pltpu.VMEM((tm, HV_C), BF16)],
        compiler_params=_params("arbitrary"),
        name="gla_out",
    )(*of_args, *ob_args, y, g_out.reshape(1, -1), w, x, *args)


def _with_cache(cache, new):
    w = new.shape[-1]
    cat = jnp.concatenate([cache.reshape(N_SAMPLE_SEQ, PAST, w).astype(BF16),
                           new.reshape(N_SAMPLE_SEQ, SAMPLE_LEN, w).astype(BF16)], axis=1)
    return cat.reshape(N_SAMPLE_SEQ * (PAST + SAMPLE_LEN), w)


def kernel(x_prompt, x_sample, cache_mla_ckv, cache_mla_krope, cache_gqa_k, cache_gqa_v, state_gla_fwd, state_gla_bwd, c, c_ctx, w_mod, b_mod, g_pre_mix, g_post_mix, g_pre_ffn, g_post_ffn, w_in_ab, g_mla_q, g_mla_kv, w_mla_uq, w_mla_ukv, g_gqa_q, g_gqa_k, w_out_ab, w_in_c, w_gate_fwd_a, w_gate_fwd_b, b_gate_fwd, w_gate_bwd_a, w_gate_bwd_b, b_gate_bwd, g_gla_out, w_out_c, w_ffn_up, ffn_conv_w, ffn_conv_b, w_ffn_down):
    depth = w_mod.shape[0]
    cond = jnp.concatenate([c_ctx[None], c, jnp.zeros((N_MOD_ROWS - 1 - N_SAMPLE_SEQ, D), F32)], axis=0)
    mod = _modulation(cond, w_mod, b_mod).reshape(depth * N_MOD_ROWS * 6, 1, D)
    w_out_ab_bf, w_out_c_bf, w_down_bf = w_out_ab.astype(BF16), w_out_c.astype(BF16), w_ffn_down.astype(BF16)
    t_sample = PAST + SAMPLE_LEN

    x = (x_prompt.reshape(NP, D), x_sample.reshape(NS, D))
    h = _prenorm(x, g_pre_mix[0], mod, 0)
    caches, states = [], []
    for i in range(depth):
        j = i // 2
        ffn_pre = (g_pre_ffn[i], i, 3, 4)
        if i % 2 == 0:
            w = w_in_ab[j]
            w_in = jnp.concatenate(
                [w[:, 0:768], w[:, 832:2368], w[:, 768:832], jnp.zeros((D, IN_AB_PAD - 2368), F32)], axis=1)
            wq = w_mla_uq[j].reshape(Q_LORA, H_A, NOPE_A + ROPE_A)
            w_uq = jnp.concatenate([wq, jnp.zeros((Q_LORA, H_A, 256 - NOPE_A - ROPE_A), F32)],
                                   axis=2).reshape(Q_LORA, H_A * 256)
            y = _matmul(h, w_in, tm=1024, tn=IN_AB_PAD // 2, out_dtype=F32, name="in_ab")
            qa, qb, ckv, kr, kb, vb = _ab_post(y, g_mla_q[j], g_mla_kv[j], g_gqa_q[j], g_gqa_k[j], w_uq)
            ckv_all = jnp.concatenate([_with_cache(cache_mla_ckv[:, j], ckv[NP:]), ckv[:NP].astype(BF16)], axis=0)
            w_ukv = w_mla_ukv[j].reshape(KV_LORA, H_A, NOPE_A + V_A)
            kn = _matmul(ckv_all, w_ukv[:, :, :NOPE_A].reshape(KV_LORA, H_A * NOPE_A),
                         tm=1024, tn=H_A * NOPE_A, out_dtype=BF16, name="k_up")
            vat = _matmul_t(w_ukv[:, :, NOPE_A:].reshape(KV_LORA, H_A * V_A).T, ckv_all,
                            tm=1024, out_dtype=BF16, name="v_up")
            o_p = _attention(qa, qb, kn, kr[:NP].astype(BF16), kb[:NP].astype(BF16), vat,
                             vb[:NP].astype(BF16).T,
                             n_seq=N_PROMPT_SEQ, s_len=PROMPT_LEN, t_len=PROMPT_LEN, row0=0,
                             kv_row0=N_SAMPLE_SEQ * t_sample, name="attn_prompt")
            kr_cache = jnp.pad(cache_mla_krope[:, j], ((0, 0), (0, 0), (0, 128 - ROPE_A)))
            o_s = _attention(qa, qb, kn, _with_cache(kr_cache, kr[NP:]),
                             _with_cache(cache_gqa_k[:, j], kb[NP:]), vat,
                             _with_cache(cache_gqa_v[:, j], vb[NP:]).T,
                             n_seq=N_SAMPLE_SEQ, s_len=SAMPLE_LEN, t_len=t_sample, row0=NP, kv_row0=0,
                             name="attn_sample")
            x, h = _proj_res((o_p, o_s), w_out_ab_bf, j, x, mod, i, 2, g_post_mix[i], ffn_pre, tm=512,
                             name="out_ab")
            caches.append((ckv[:NP].reshape(N_PROMPT_SEQ, PROMPT_LEN, KV_LORA),
                           kr[:NP, :ROPE_A].reshape(N_PROMPT_SEQ, PROMPT_LEN, ROPE_A),
                           kb[:NP].reshape(N_PROMPT_SEQ, PROMPT_LEN, KV_B, HD_B),
                           vb[:NP].reshape(N_PROMPT_SEQ, PROMPT_LEN, KV_B, HD_B)))
        else:
            y = _matmul(h, w_in_c[j], tm=1024, tn=768, out_dtype=BF16, name="in_c")
            w_z = jnp.concatenate([w_gate_fwd_a[j], w_gate_bwd_a[j], jnp.zeros((D, 128 - 2 * GATE_RANK), F32)], axis=1)
            z = _matmul(h, w_z, tm=1024, tn=128, out_dtype=F32, name="gate_lowrank")
            gate_args = (w_gate_fwd_b[j], b_gate_fwd[j], w_gate_bwd_b[j], b_gate_bwd[j])
            of_p, ob_p, sf, sb = _gla(y, z, *gate_args, None, None,
                                      n_seq=N_PROMPT_SEQ, s_len=PROMPT_LEN, row0=0, name="gla_prompt")
            of_s, ob_s, _, _ = _gla(y, z, *gate_args, state_gla_fwd[:, j], state_gla_bwd[:, j],
                                    n_seq=N_SAMPLE_SEQ, s_len=SAMPLE_LEN, row0=NP, name="gla_sample")
            x, h = _gla_out((of_p, of_s), (ob_p, ob_s), y, g_gla_out[j], w_out_c_bf, j, x, mod, i,
                            g_post_mix[i], ffn_pre)
            states.append((sf, sb))
        act = _ffn_up(h, w_ffn_up, ffn_conv_w, ffn_conv_b, i)
        if i + 1 < depth:
            nxt = (g_pre_mix[i + 1], i + 1, 0, 1)
            x, h = _proj_res(act, w_down_bf, i, x, mod, i, 5, g_post_ffn[i], nxt, tm=256, name="ffn_down")
        else:
            y_prompt, _ = _proj_res(act, w_down_bf, i, x, mod, i, 5, g_post_ffn[i], None, tm=256,
                                    name="ffn_down_prompt", row0=0, n_rows=NP)
            y_sample, _ = _proj_res(act, w_down_bf, i, x, mod, i, 5, g_post_ffn[i], None, tm=256,
                                    name="ffn_down_sample", row0=NP, n_rows=NS)

    new_ckv, new_krope, new_k, new_v = (jnp.stack(t, axis=1) for t in zip(*caches))
    new_sf, new_sb = (jnp.stack(t, axis=1) for t in zip(*states))
    return (y_prompt.reshape(N_PROMPT_SEQ, PROMPT_LEN, D), y_sample.reshape(N_SAMPLE_SEQ, SAMPLE_LEN, D),
            new_ckv, new_krope, new_k, new_v, new_sf, new_sb)
```

```python
import functools
import math

import numpy as np
import jax
import jax.numpy as jnp
from jax import lax
from jax.experimental import pallas as pl
from jax.experimental.pallas import tpu as pltpu

D = 2048
N_PROMPT_SEQ, PROMPT_LEN = 16, 256
N_SAMPLE_SEQ, SAMPLE_LEN = 4, 2048
PAST = 512
GRID_W = 64
NP = N_PROMPT_SEQ * PROMPT_LEN
NS = N_SAMPLE_SEQ * SAMPLE_LEN
M_ROWS = NP + NS
EPS = 1e-6
ROPE_THETA = 10000.0

H_A, NOPE_A, ROPE_A, V_A = 8, 128, 64, 128
Q_LORA, KV_LORA = 512, 256
H_B, KV_B, HD_B = 8, 2, 128
IN_AB_PAD = 2560
H_C, DK_C, DV_C = 4, 256, 512
HK_C, HV_C = H_C * DK_C, H_C * DV_C
GATE_RANK = 16
GATE_TAU = 16.0
D_FF = 5632
GLA_CHUNK = 128
GLA_FINE = 3
N_MOD_ROWS = 8

VMEM_LIMIT_BYTES = 56 * 1024 * 1024
BF16 = jnp.bfloat16
F32 = jnp.float32


def _params(*sem):
    return pltpu.CompilerParams(dimension_semantics=sem, vmem_limit_bytes=VMEM_LIMIT_BYTES)


def _mod_row(i, tm):
    n_prompt_tiles = NP // tm
    per_seq = SAMPLE_LEN // tm
    return jnp.where(i < n_prompt_tiles, 0, 1 + (i - n_prompt_tiles) // per_seq)


def _mod_spec(layer, piece, tm, tile0=0):
    def index_map(i, *_):
        return ((layer * N_MOD_ROWS + _mod_row(tile0 + i, tm)) * 6 + piece, 0, 0)
    return pl.BlockSpec((None, 1, D), index_map)


def _rms(x, g):
    return x * lax.rsqrt(jnp.mean(x * x, axis=-1, keepdims=True) + EPS) * g


def _dot(a, b):
    return jnp.dot(a, b, preferred_element_type=F32)


def _dot_nt(a, b):
    return lax.dot_general(a, b, (((1,), (1,)), ((), ())), preferred_element_type=F32)


def _split2(x):
    hi = x.astype(BF16)
    lo = (x - hi.astype(F32)).astype(BF16)
    return hi, lo


def _mod_kernel(c_ref, w_ref, b_ref, o_ref):
    c = c_ref[...]
    a = c * (1.0 / (1.0 + jnp.exp(-c)))
    a_hi, a_lo = _split2(a)
    w_hi, w_lo = _split2(w_ref[...])
    r = _dot(jnp.concatenate([a_hi, a_lo], axis=0), w_hi)
    o_ref[...] = r[:N_MOD_ROWS] + r[N_MOD_ROWS:] + _dot(a_hi, w_lo) + b_ref[...]


def _modulation(cond, w_mod, b_mod):
    depth = w_mod.shape[0]
    n = w_mod.shape[2]
    tn = 1024
    return pl.pallas_call(
        _mod_kernel,
        grid=(depth, n // tn),
        in_specs=[pl.BlockSpec((N_MOD_ROWS, D), lambda l, j: (0, 0)),
                  pl.BlockSpec((None, D, tn), lambda l, j: (l, 0, j)),
                  pl.BlockSpec((None, 1, tn), lambda l, j: (l, 0, j))],
        out_specs=pl.BlockSpec((None, N_MOD_ROWS, tn), lambda l, j: (l, 0, j)),
        out_shape=jax.ShapeDtypeStruct((depth, N_MOD_ROWS, n), F32),
        compiler_params=_params("arbitrary", "arbitrary"),
        name="modulation",
    )(cond, w_mod, b_mod.reshape(depth, 1, n))


def _matmul_kernel(x_ref, w_ref, o_ref):
    o_ref[...] = _dot(x_ref[...].astype(BF16), w_ref[...].astype(BF16)).astype(o_ref.dtype)


def _matmul(x, w, *, tm, tn, out_dtype, name):
    m, k = x.shape
    n = w.shape[1]
    return pl.pallas_call(
        _matmul_kernel,
        grid=(n // tn, m // tm),
        in_specs=[pl.BlockSpec((tm, k), lambda j, i: (i, 0)),
                  pl.BlockSpec((k, tn), lambda j, i: (0, j))],
        out_specs=pl.BlockSpec((tm, tn), lambda j, i: (i, j)),
        out_shape=jax.ShapeDtypeStruct((m, n), out_dtype),
        compiler_params=_params("arbitrary", "arbitrary"),
        name=name,
    )(x, w)


def _rope_tables(rot_dim, tm):
    q = rot_dim // 4
    inv = ROPE_THETA ** (-np.arange(q, dtype=np.float64) / q)
    tok = np.arange(SAMPLE_LEN)
    ang_r = (tok // GRID_W)[:, None] * inv
    ang_c = (tok % GRID_W)[:, None] * inv
    ang = np.concatenate([ang_r, ang_r, ang_c, ang_c], axis=-1)
    cos, sin = np.cos(ang), np.sin(ang)
    first = (np.arange(rot_dim) % (2 * q)) < q
    sin_m = np.where(first, -sin, 0.0)
    sin_p = np.where(first, 0.0, sin)

    def widen(t, ident):
        full = np.full((tm + SAMPLE_LEN, 128), ident, np.float64)
        full[tm:, :rot_dim] = t
        return jnp.asarray(full, F32)

    return widen(cos, 1.0), widen(sin_m, 0.0), widen(sin_p, 0.0), q


def _rope128(x, cos, sin_m, sin_p, q):
    return (x * cos + pltpu.roll(x, 128 - q, 1) * sin_m + pltpu.roll(x, q, 1) * sin_p)


def _ab_post_kernel(y_ref, gq_ref, gkv_ref, gqn_ref, gkn_ref, wuq_ref,
                    ca_ref, ma_ref, pa_ref, cb_ref, mb_ref, pb_ref,
                    qa_ref, qb_ref, ckv_ref, kr_ref, kb_ref, vb_ref, *, qa_rot, qb_rot):
    ca, ma, pa = ca_ref[...], ma_ref[...], pa_ref[...]
    cb, mb, pb = cb_ref[...], mb_ref[...], pb_ref[...]
    scale_a = (NOPE_A + ROPE_A) ** -0.5 * math.log2(math.e)
    scale_b = HD_B ** -0.5 * math.log2(math.e)

    cq = _rms(y_ref[:, 0:Q_LORA], gq_ref[...])
    qa = _dot(cq.astype(BF16), wuq_ref[...].astype(BF16)) * scale_a
    for h in range(H_A):
        base = h * 256
        qa_ref[:, base:base + 128] = qa[:, base:base + 128].astype(qa_ref.dtype)
        qa_ref[:, base + 128:base + 256] = _rope128(
            qa[:, base + 128:base + 256], ca, ma, pa, qa_rot).astype(qa_ref.dtype)

    ckv_ref[...] = _rms(y_ref[:, 512:768], gkv_ref[...])
    kr_ref[...] = _rope128(y_ref[:, 2304:2432], ca, ma, pa, qa_rot)

    for h in range(H_B):
        xb = _rms(y_ref[:, 768 + h * 128:768 + (h + 1) * 128], gqn_ref[...])
        qb_ref[:, h * 128:(h + 1) * 128] = (
            _rope128(xb, cb, mb, pb, qb_rot) * scale_b).astype(qb_ref.dtype)
    for g in range(KV_B):
        xk = _rms(y_ref[:, 1792 + g * 128:1792 + (g + 1) * 128], gkn_ref[...])
        kb_ref[:, g * 128:(g + 1) * 128] = _rope128(xk, cb, mb, pb, qb_rot)
    vb_ref[...] = y_ref[:, 2048:2304]


def _ab_in_kernel(*refs, n_x, qa_rot, qb_rot):
    x_refs = refs[:n_x]
    g_ref, sh_ref, sc_ref, win_ref = refs[n_x:n_x + 4]
    y_ref = refs[-1]
    h = (_rms(_row_load(list(x_refs)), g_ref[...]) * (1.0 + sc_ref[...]) + sh_ref[...]).astype(BF16)
    y_ref[...] = _dot(h, win_ref[...])
    _ab_post_kernel(y_ref, *refs[n_x + 4:-1], qa_rot=qa_rot, qb_rot=qb_rot)


def _ab_in(x, g_pre, mod, layer, w_in_bf, g_q, g_kv, g_qn, g_kn, w_uq_bf):
    tm = 256
    ca, ma, pa, qa_rot = _rope_tables(ROPE_A, tm)
    cb, mb, pb, qb_rot = _rope_tables(HD_B, tm)
    n_prompt_tiles = NP // tm
    per_seq = SAMPLE_LEN // tm

    def tab_map(i):
        return (jnp.where(i < n_prompt_tiles, 0, 1 + (i - n_prompt_tiles) % per_seq), 0)

    tab = pl.BlockSpec((tm, 128), tab_map)
    row = lambda w: pl.BlockSpec((tm, w), lambda i: (i, 0))
    vec = lambda w: pl.BlockSpec((1, w), lambda i: (0, 0))
    resident = lambda arr: pl.BlockSpec(arr.shape, lambda i: (0, 0), pipeline_mode=pl.Buffered(1))
    x_specs, x_args = _row_specs(x, tm)
    return pl.pallas_call(
        functools.partial(_ab_in_kernel, n_x=len(x_args), qa_rot=qa_rot, qb_rot=qb_rot),
        grid=(M_ROWS // tm,),
        in_specs=x_specs + [vec(D), _mod_spec(layer, 0, tm), _mod_spec(layer, 1, tm), resident(w_in_bf),
                            vec(Q_LORA), vec(KV_LORA), vec(HD_B), vec(HD_B), resident(w_uq_bf),
                            tab, tab, tab, tab, tab, tab],
        out_specs=[row(H_A * 256), row(H_B * HD_B), row(KV_LORA), row(128),
                   row(KV_B * HD_B), row(KV_B * HD_B)],
        out_shape=[jax.ShapeDtypeStruct((M_ROWS, H_A * 256), BF16),
                   jax.ShapeDtypeStruct((M_ROWS, H_B * HD_B), BF16),
                   jax.ShapeDtypeStruct((M_ROWS, KV_LORA), F32),
                   jax.ShapeDtypeStruct((M_ROWS, 128), F32),
                   jax.ShapeDtypeStruct((M_ROWS, KV_B * HD_B), F32),
                   jax.ShapeDtypeStruct((M_ROWS, KV_B * HD_B), F32)],
        scratch_shapes=[pltpu.VMEM((tm, IN_AB_PAD), F32)],
        compiler_params=_params("arbitrary"),
        name="ab_in",
    )(*x_args, g_pre.reshape(1, D), mod, mod, w_in_bf, g_q.reshape(1, -1), g_kv.reshape(1, -1),
      g_qn.reshape(1, -1), g_kn.reshape(1, -1), w_uq_bf, ca, ma, pa, cb, mb, pb)


def _softmax_pv(s, v):
    m = jnp.max(s, axis=-1, keepdims=True)
    e = jnp.exp2(s - m)
    l = jnp.sum(e, axis=-1, keepdims=True)
    return _dot(e.astype(BF16), v) / l


def _attn_kernel(qa_ref, qb_ref, kv_ref, kr_ref, kb_ref, vb_ref, *rest):
    o_ref = rest[-1]
    kr = kr_ref[...]
    rep = H_B // KV_B
    n_heads = H_A + H_B

    def scores(h):
        if h < H_A:
            k = jnp.concatenate([kv_ref[:, h * 256:h * 256 + 128], kr], axis=1)
            return _dot_nt(qa_ref[:, h * 256:(h + 1) * 256], k)
        g = (h - H_A) // rep
        return _dot_nt(qb_ref[:, (h - H_A) * 128:(h - H_A + 1) * 128], kb_ref[:, g * 128:(g + 1) * 128])

    def values(h):
        if h < H_A:
            return kv_ref[:, h * 256 + 128:(h + 1) * 256]
        g = (h - H_A) // rep
        return vb_ref[:, g * 128:(g + 1) * 128]

    s_next = scores(0)
    for h in range(n_heads):
        s_cur = s_next
        if h + 1 < n_heads:
            s_next = scores(h + 1)
        o_ref[:, h * 128:(h + 1) * 128] = _softmax_pv(s_cur, values(h)).astype(o_ref.dtype)


def _attention(qa, qb, kv, kr, kb, vb, *, n_seq, s_len, t_len, row0, kv_row0, name):
    tq = 256
    nq = s_len // tq
    q0 = row0 // tq
    kv0 = kv_row0 // t_len
    qspec = lambda w: pl.BlockSpec((tq, w), lambda b, i: (q0 + b * nq + i, 0))
    kspec = lambda w, b0=0: pl.BlockSpec((t_len, w), lambda b, i: (b0 + b, 0), pipeline_mode=pl.Buffered(1))
    return pl.pallas_call(
        _attn_kernel,
        grid=(n_seq, nq),
        in_specs=[qspec(H_A * 256), qspec(H_B * HD_B), kspec(H_A * 256, kv0), kspec(128),
                  kspec(KV_B * HD_B), kspec(KV_B * HD_B)],
        out_specs=pl.BlockSpec((tq, D), lambda b, i: (b * nq + i, 0)),
        out_shape=jax.ShapeDtypeStruct((n_seq * s_len, D), BF16),
        compiler_params=_params("arbitrary", "arbitrary"),
        name=name,
    )(qa, qb, kv, kr, kb, vb)


def _row_specs(arr, tm, tile0=0):
    if not isinstance(arr, tuple):
        return [pl.BlockSpec((tm, arr.shape[1]), lambda i, *_: (tile0 + i, 0))], [arr]
    npt = NP // tm
    w = arr[0].shape[1]
    return ([pl.BlockSpec((tm, w), lambda i, *_: (jnp.minimum(tile0 + i, npt - 1), 0)),
             pl.BlockSpec((tm, w), lambda i, *_: (jnp.maximum(tile0 + i - npt, 0), 0))], list(arr))


def _row_load(refs, tile0=0):
    if len(refs) == 1:
        return refs[0][...]
    tm = refs[0].shape[0]
    return jnp.where(tile0 + pl.program_id(0) < NP // tm, refs[0][...], refs[1][...])


def _residual_epilogue(acc, x, gpost_ref, gate_ref, nxt, xo_ref, ho_ref):
    x_new = x + gate_ref[...] * _rms(acc, gpost_ref[...])
    xo_ref[...] = x_new
    if nxt is not None:
        gn_ref, sh_ref, sc_ref = nxt
        ho_ref[...] = (_rms(x_new, gn_ref[...]) * (1.0 + sc_ref[...]) + sh_ref[...]).astype(ho_ref.dtype)


def _proj_res_kernel(*refs, n_a, n_x, emit_h, tile0):
    a_refs, w_ref, x_refs, rest = refs[:n_a], refs[n_a], refs[n_a + 1:n_a + 1 + n_x], refs[n_a + 1 + n_x:]
    if emit_h:
        gpost_ref, gate_ref, gn_ref, sh_ref, sc_ref, xo_ref, ho_ref = rest
        nxt = (gn_ref, sh_ref, sc_ref)
    else:
        gpost_ref, gate_ref, xo_ref = rest
        nxt, ho_ref = None, None
    acc = _dot(_row_load(a_refs, tile0), w_ref[...])
    _residual_epilogue(acc, _row_load(x_refs, tile0), gpost_ref, gate_ref, nxt, xo_ref, ho_ref)


def _res_specs(tm, tile0, n_rows, mod, layer, gate_piece, g_post, nxt):
    vec = pl.BlockSpec((1, D), lambda i, *_: (0, 0))
    specs = [vec, _mod_spec(layer, gate_piece, tm, tile0)]
    args = [g_post.reshape(1, D), mod]
    if nxt is not None:
        g_next, layer_next, sh_piece, sc_piece = nxt
        specs += [vec, _mod_spec(layer_next, sh_piece, tm, tile0), _mod_spec(layer_next, sc_piece, tm, tile0)]
        args += [g_next.reshape(1, D), mod, mod]
    out_specs = [pl.BlockSpec((tm, D), lambda i, *_: (i, 0))]
    out_shape = [jax.ShapeDtypeStruct((n_rows, D), F32)]
    if nxt is not None:
        out_specs.append(pl.BlockSpec((tm, D), lambda i, *_: (i, 0)))
        out_shape.append(jax.ShapeDtypeStruct((n_rows, D), BF16))
    return specs, args, out_specs, out_shape


def _resident_weight_spec(kdim, w_layer):
    return pl.BlockSpec((None, kdim, D), lambda i: (w_layer, 0, 0), pipeline_mode=pl.Buffered(1))


def _proj_res(a, w, w_layer, x, mod, layer, gate_piece, g_post, nxt, *, tm, name, row0=0, n_rows=M_ROWS):
    tile0 = row0 // tm
    a_specs, a_args = _row_specs(a, tm, tile0)
    x_specs, x_args = _row_specs(x, tm, tile0)
    kdim = a_args[0].shape[1]
    specs, args, out_specs, out_shape = _res_specs(tm, tile0, n_rows, mod, layer, gate_piece, g_post, nxt)
    out = pl.pallas_call(
        functools.partial(_proj_res_kernel, n_a=len(a_args), n_x=len(x_args), emit_h=nxt is not None, tile0=tile0),
        grid=(n_rows // tm,),
        in_specs=a_specs + [_resident_weight_spec(kdim, w_layer)] + x_specs + specs,
        out_specs=out_specs,
        out_shape=out_shape,
        compiler_params=_params("arbitrary"),
        name=name,
    )(*a_args, w, *x_args, *args)
    return out if nxt is not None else (out[0], None)


def _ffn_up_kernel(h_ref, hp_ref, hn_ref, wa_ref, wg_ref, cwa_ref, cwg_ref, cba_ref, cbg_ref,
                   mp_ref, mn_ref, o_ref, wbf_ref):
    tm = h_ref.shape[0]
    halo = hp_ref.shape[0]

    @pl.when(pl.program_id(1) == 0)
    def _():
        wbf_ref[0] = wa_ref[...].astype(BF16)
        wbf_ref[1] = wg_ref[...].astype(BF16)

    h = h_ref[...]
    edge = jnp.concatenate([hp_ref[...], hn_ref[...]], axis=0)
    row8 = lax.broadcasted_iota(jnp.int32, (8, 1), 0)
    n_seg = tm // PROMPT_LEN

    def patch_edges(shifted, tile_edge_row, at_start, mask_ref):
        pieces = []
        for seg in range(n_seg):
            r = seg * PROMPT_LEN if at_start else (seg + 1) * PROMPT_LEN - 1
            g0 = r - r % 8
            grp = shifted[g0:g0 + 8]
            at_tile_edge = seg == 0 if at_start else seg == n_seg - 1
            src = tile_edge_row if at_tile_edge else grp
            grp = jnp.where(row8 == r % 8, src * mask_ref[r:r + 1, :], grp)
            if at_start:
                pieces += [grp, shifted[g0 + 8:(seg + 1) * PROMPT_LEN]]
            else:
                pieces += [shifted[seg * PROMPT_LEN:g0], grp]
        return jnp.concatenate(pieces, axis=0)

    def branch(which, cw_ref, cb_ref):
        w = wbf_ref[which]
        u = _dot(h, w)
        u_edge = _dot(edge, w)
        prev = patch_edges(pltpu.roll(u, 1, 0), u_edge[halo - 1:halo], True, mp_ref)
        nxt = patch_edges(pltpu.roll(u, tm - 1, 0), u_edge[halo:halo + 1], False, mn_ref)
        cw = cw_ref[...]
        return prev * cw[0:1] + u * cw[1:2] + nxt * cw[2:3] + cb_ref[...]

    a = branch(0, cwa_ref, cba_ref)
    g = branch(1, cwg_ref, cbg_ref)
    o_ref[...] = (g * (1.0 / (1.0 + jnp.exp(-g))) * a).astype(o_ref.dtype)


def _seq_edge_masks():
    pos = np.concatenate([np.tile(np.arange(PROMPT_LEN), N_PROMPT_SEQ),
                          np.tile(np.arange(SAMPLE_LEN), N_SAMPLE_SEQ)])
    lens = np.concatenate([np.full(NP, PROMPT_LEN), np.full(NS, SAMPLE_LEN)])
    has_prev = (pos > 0).astype(np.float32).reshape(M_ROWS, 1)
    has_next = (pos < lens - 1).astype(np.float32).reshape(M_ROWS, 1)
    return jnp.asarray(has_prev), jnp.asarray(has_next)


def _ffn_up(h, w_up, conv_w, conv_b, layer):
    tm, tn, halo = 1024, 512, 16
    nj = D_FF // tn
    r = tm // halo
    last = M_ROWS // halo - 1
    has_prev, has_next = _seq_edge_masks()
    conv_b = conv_b.reshape(conv_b.shape[0], 1, 2 * D_FF)
    col = lambda rows, half: pl.BlockSpec((None, rows, tn), lambda j, i: (layer, 0, j + half * nj))
    return pl.pallas_call(
        _ffn_up_kernel,
        grid=(nj, M_ROWS // tm),
        in_specs=[pl.BlockSpec((tm, D), lambda j, i: (i, 0)),
                  pl.BlockSpec((halo, D), lambda j, i: (jnp.maximum(i * r - 1, 0), 0)),
                  pl.BlockSpec((halo, D), lambda j, i: (jnp.minimum((i + 1) * r, last), 0)),
                  col(D, 0), col(D, 1), col(3, 0), col(3, 1), col(1, 0), col(1, 1),
                  pl.BlockSpec((tm, 1), lambda j, i: (i, 0)),
                  pl.BlockSpec((tm, 1), lambda j, i: (i, 0))],
        out_specs=pl.BlockSpec((tm, tn), lambda j, i: (i, j)),
        out_shape=jax.ShapeDtypeStruct((M_ROWS, D_FF), BF16),
        scratch_shapes=[pltpu.VMEM((2, D, tn), BF16)],
        compiler_params=_params("arbitrary", "arbitrary"),
        name="ffn_up",
    )(h, h, h, w_up, w_up, conv_w, conv_w, conv_b, conv_b, has_prev, has_next)


def _gla_masks(c):
    levels = int(math.log2(c))
    t = np.arange(c)
    pair = [t[:, None] == t[None, :]]
    for l in range(1, levels + 1):
        blk = t >> l
        up = ((t >> (l - 1)) & 1) == 1
        pair.append((blk[:, None] == blk[None, :]) & up[:, None] & ~up[None, :])
    pair = np.stack(pair).astype(np.float32)
    pair = np.stack([pair, pair.transpose(0, 2, 1)])
    mats = []
    for l in range(1, GLA_FINE + 1):
        split = ((t >> l) << l) + (1 << (l - 1)) - 1
        lo, hi = np.minimum(t, split), np.maximum(t, split)
        mats.append((t[None, :] > lo[:, None]) & (t[None, :] <= hi[:, None]))
    same = (t >> GLA_FINE)[:, None] == (t >> GLA_FINE)[None, :]
    mats += [same & (t[None, :] <= t[:, None]), same]
    fine = np.stack([np.concatenate(mats, axis=0),
                     np.concatenate([m[::-1, ::-1] for m in mats], axis=0)]).astype(np.float32)
    return jnp.asarray(pair), jnp.asarray(fine, BF16)


def _gla_chain(q, k, v, za, wb, ba, pair_ref, fine_ref, s_ref, reverse):
    c = q.shape[0]
    levels = pair_ref.shape[1] - 1
    za_hi, za_lo = _split2(za)
    wb_hi, wb_lo = _split2(wb)
    z = _dot(za_hi, wb_hi) + _dot(za_lo, wb_hi) + _dot(za_hi, wb_lo) + ba
    x = (jnp.minimum(z, 0.0) - jnp.log(1.0 + jnp.exp(-jnp.abs(z)))) * (math.log2(math.e) / GATE_TAU)
    yield None

    rev = 1 if reverse else 0
    first_i, second_i = (1, 0) if reverse else (0, 1)
    x_hi, x_lo = _split2(x)
    fine = _dot(fine_ref[rev], jnp.concatenate([x_hi, x_lo], axis=1))
    yield None
    fine = fine[:, :DK_C] + fine[:, DK_C:]
    p, t = fine[GLA_FINE * c:(GLA_FINE + 1) * c], fine[(GLA_FINE + 1) * c:(GLA_FINE + 2) * c]
    k_bf = k.astype(BF16)
    a = pair_ref[rev, 0] * _dot_nt(q.astype(BF16), k_bf)
    for l in range(1, levels + 1):
        s = 1 << (l - 1)
        if l <= GLA_FINE:
            e = jnp.exp2(fine[(l - 1) * c:l * c])
            a = a + pair_ref[rev, l] * _dot_nt((q * e).astype(BF16), (k * e).astype(BF16))
        else:
            nb = c // (2 * s)
            split = lambda arr: arr.reshape(nb, 2, s, arr.shape[-1])
            join = lambda f, sec: jnp.stack([f, sec] if first_i == 0 else [sec, f], axis=1)
            p4, t4 = split(p), split(t)
            p_f, p_s, t_f, t_s = p4[:, first_i], p4[:, second_i], t4[:, first_i], t4[:, second_i]
            q_s = split(q)[:, second_i] * jnp.exp2(p_s)
            k_f = split(k)[:, first_i] * jnp.exp2(t_f - p_f)
            k_all = join(k_f.astype(BF16), split(k_bf)[:, second_i]).reshape(c, DK_C)
            r = _dot_nt(q_s.reshape(c // 2, DK_C).astype(BF16), k_all).reshape(nb, s, c)
            a4 = a.reshape(nb, 2, s, c)
            mask = pair_ref[rev, l].reshape(nb, 2, s, c)[:, second_i]
            a = join(a4[:, first_i], a4[:, second_i] + mask * r).reshape(c, c)
            t_new = t_f + t_s
            p = join(p_f, p_s + t_f).reshape(c, DK_C)
            t = join(t_new, t_new).reshape(c, DK_C)
        yield None
    s_old = s_ref[...]
    o_inter = _dot((q * jnp.exp2(p)).astype(BF16), s_old.astype(BF16))
    kd_t = (k * jnp.exp2(t - p)).T.astype(BF16)
    both = _dot(jnp.concatenate([a.astype(BF16), kd_t], axis=0), v)
    dec_col = jnp.exp2(t[0:8]).T[:, 0:1]
    s_ref[...] = s_old * dec_col + both[c:]
    yield both[:c] + o_inter


def _gla_kernel(*refs, zero_init):
    (qf_ref, kf_ref, vf_ref, zf_ref, qb_ref, kb_ref, vb_ref, zb_ref,
     wbf_ref, baf_ref, wbb_ref, bab_ref, pair_ref, fine_ref) = refs[:14]
    rest = refs[14:]
    if not zero_init:
        sfi_ref, sbi_ref = rest[:2]
        rest = rest[2:]
    of_ref, ob_ref, sfo_ref, sbo_ref, sf_ref, sb_ref = rest
    ci = pl.program_id(1)

    @pl.when(ci == 0)
    def _():
        if zero_init:
            sf_ref[...] = jnp.zeros_like(sf_ref)
            sb_ref[...] = jnp.zeros_like(sb_ref)
        else:
            sf_ref[...] = sfi_ref[...]
            sb_ref[...] = sbi_ref[...]

    scale = DK_C ** -0.5
    chains = []
    for h in range(H_C):
        ks, vs = slice(h * DK_C, (h + 1) * DK_C), slice(h * DV_C, (h + 1) * DV_C)
        chains.append((of_ref, vs, _gla_chain(
            qf_ref[:, ks].astype(F32) * scale, kf_ref[:, ks].astype(F32), vf_ref[:, vs],
            zf_ref[:, 0:GATE_RANK], wbf_ref[:, ks], baf_ref[:, ks], pair_ref, fine_ref, sf_ref.at[h], False)))
        chains.append((ob_ref, vs, _gla_chain(
            qb_ref[:, ks].astype(F32) * scale, kb_ref[:, ks].astype(F32), vb_ref[:, vs],
            zb_ref[:, GATE_RANK:2 * GATE_RANK], wbb_ref[:, ks], bab_ref[:, ks], pair_ref, fine_ref,
            sb_ref.at[h], True)))
    live = True
    while live:
        for o_ref, vs, chain in chains:
            out = next(chain, "done")
            if isinstance(out, str):
                live = False
            elif out is not None:
                o_ref[:, vs] = out.astype(o_ref.dtype)

    @pl.when(ci == pl.num_programs(1) - 1)
    def _():
        sfo_ref[...] = sf_ref[...]
        sbo_ref[...] = sb_ref[...]


def _gla(y, z, wb_f, ba_f, wb_b, ba_b, s_f, s_b, *, n_seq, s_len, row0, name):
    c = GLA_CHUNK
    n = s_len // c
    r0 = row0 // c
    zero_init = s_f is None
    pair, fine = _gla_masks(c)
    fwd = lambda b, i: b * n + i
    bwd = lambda b, i: b * n + (n - 1 - i)

    def dir_specs(rows):
        return [pl.BlockSpec((c, HK_C), lambda b, i: (r0 + rows(b, i), 0)),
                pl.BlockSpec((c, HK_C), lambda b, i: (r0 + rows(b, i), 1)),
                pl.BlockSpec((c, HV_C), lambda b, i: (r0 + rows(b, i), 1)),
                pl.BlockSpec((c, 128), lambda b, i: (r0 + rows(b, i), 0))]

    full = lambda arr: pl.BlockSpec(arr.shape, lambda b, i: (0,) * arr.ndim)
    state = pl.BlockSpec((None, H_C, DK_C, DV_C), lambda b, i: (b, 0, 0, 0))
    ba_f, ba_b = ba_f.reshape(1, -1), ba_b.reshape(1, -1)
    in_specs = (dir_specs(fwd) + dir_specs(bwd)
                + [full(wb_f), full(ba_f), full(wb_b), full(ba_b), full(pair), full(fine)])
    args = [y, y, y, z, y, y, y, z, wb_f, ba_f, wb_b, ba_b, pair, fine]
    if not zero_init:
        in_specs += [state, state]
        args += [s_f, s_b]
    return pl.pallas_call(
        functools.partial(_gla_kernel, zero_init=zero_init),
        grid=(n_seq, n),
        in_specs=in_specs,
        out_specs=[pl.BlockSpec((c, HV_C), lambda b, i: (fwd(b, i), 0)),
                   pl.BlockSpec((c, HV_C), lambda b, i: (bwd(b, i), 0)),
                   state, state],
        out_shape=[jax.ShapeDtypeStruct((n_seq * s_len, HV_C), BF16),
                   jax.ShapeDtypeStruct((n_seq * s_len, HV_C), BF16),
                   jax.ShapeDtypeStruct((n_seq, H_C, DK_C, DV_C), F32),
                   jax.ShapeDtypeStruct((n_seq, H_C, DK_C, DV_C), F32)],
        scratch_shapes=[pltpu.VMEM((H_C, DK_C, DV_C), F32), pltpu.VMEM((H_C, DK_C, DV_C), F32)],
        compiler_params=_params("arbitrary", "arbitrary"),
        name=name,
    )(*args)


def _gla_out_kernel(ofp_ref, ofs_ref, obp_ref, obs_ref, r_ref, go_ref, w_ref, x_ref, gpost_ref, gate_ref,
                    gn_ref, sh_ref, sc_ref, xo_ref, ho_ref, a_ref):
    o_all = _row_load([ofp_ref, ofs_ref]).astype(F32) + _row_load([obp_ref, obs_ref]).astype(F32)
    for h in range(H_C):
        vs = slice(h * DV_C, (h + 1) * DV_C)
        r = r_ref[:, vs].astype(F32)
        a = _rms(o_all[:, vs], go_ref[...]) * (r * (1.0 / (1.0 + jnp.exp(-r))))
        a_ref[:, vs] = a.astype(BF16)
    _residual_epilogue(_dot(a_ref[...], w_ref[...]), x_ref[...], gpost_ref, gate_ref,
                       (gn_ref, sh_ref, sc_ref), xo_ref, ho_ref)


def _gla_out(o_f, o_b, y, g_out, w, w_layer, x, mod, layer, g_post, nxt):
    tm = 256
    r_blk = (2 * HK_C + HV_C) // HV_C
    specs, args, out_specs, out_shape = _res_specs(tm, 0, M_ROWS, mod, layer, 2, g_post, nxt)
    of_specs, of_args = _row_specs(o_f, tm)
    ob_specs, ob_args = _row_specs(o_b, tm)
    return pl.pallas_call(
        _gla_out_kernel,
        grid=(M_ROWS // tm,),
        in_specs=of_specs + ob_specs + [
            pl.BlockSpec((tm, HV_C), lambda i: (i, r_blk)),
            pl.BlockSpec((1, DV_C), lambda i: (0, 0)),
            _resident_weight_spec(HV_C, w_layer),
            pl.BlockSpec((tm, D), lambda i: (i, 0))] + specs,
        out_specs=out_specs,
        out_shape=out_shape,
        scratch_shapes=[pltpu.VMEM((tm, HV_C), BF16)],
        compiler_params=_params("arbitrary"),
        name="gla_out",
    )(*of_args, *ob_args, y, g_out.reshape(1, -1), w, x, *args)


def _with_cache(cache, new):
    w = new.shape[-1]
    cat = jnp.concatenate([cache.reshape(N_SAMPLE_SEQ, PAST, w).astype(BF16),
                           new.reshape(N_SAMPLE_SEQ, SAMPLE_LEN, w).astype(BF16)], axis=1)
    return cat.reshape(N_SAMPLE_SEQ * (PAST + SAMPLE_LEN), w)


def kernel(x_prompt, x_sample, cache_mla_ckv, cache_mla_krope, cache_gqa_k, cache_gqa_v, state_gla_fwd, state_gla_bwd, c, c_ctx, w_mod, b_mod, g_pre_mix, g_post_mix, g_pre_ffn, g_post_ffn, w_in_ab, g_mla_q, g_mla_kv, w_mla_uq, w_mla_ukv, g_gqa_q, g_gqa_k, w_out_ab, w_in_c, w_gate_fwd_a, w_gate_fwd_b, b_gate_fwd, w_gate_bwd_a, w_gate_bwd_b, b_gate_bwd, g_gla_out, w_out_c, w_ffn_up, ffn_conv_w, ffn_conv_b, w_ffn_down):
    depth = w_mod.shape[0]
    cond = jnp.concatenate([c_ctx[None], c, jnp.zeros((N_MOD_ROWS - 1 - N_SAMPLE_SEQ, D), F32)], axis=0)
    mod = _modulation(cond, w_mod, b_mod).reshape(depth * N_MOD_ROWS * 6, 1, D)
    w_out_ab_bf, w_out_c_bf, w_down_bf = w_out_ab.astype(BF16), w_out_c.astype(BF16), w_ffn_down.astype(BF16)
    t_sample = PAST + SAMPLE_LEN

    x = (x_prompt.reshape(NP, D), x_sample.reshape(NS, D))
    h = None
    caches, states = [], []
    for i in range(depth):
        j = i // 2
        ffn_pre = (g_pre_ffn[i], i, 3, 4)
        if i % 2 == 0:
            w = w_in_ab[j]
            w_in = jnp.concatenate(
                [w[:, 0:768], w[:, 832:2368], w[:, 768:832], jnp.zeros((D, IN_AB_PAD - 2368), F32)],
                axis=1).astype(BF16)
            wq = w_mla_uq[j].reshape(Q_LORA, H_A, NOPE_A + ROPE_A)
            w_uq = jnp.concatenate([wq, jnp.zeros((Q_LORA, H_A, 256 - NOPE_A - ROPE_A), F32)],
                                   axis=2).reshape(Q_LORA, H_A * 256).astype(BF16)
            qa, qb, ckv, kr, kb, vb = _ab_in(x, g_pre_mix[i], mod, i, w_in, g_mla_q[j], g_mla_kv[j],
                                             g_gqa_q[j], g_gqa_k[j], w_uq)
            ckv_all = jnp.concatenate([_with_cache(cache_mla_ckv[:, j], ckv[NP:]), ckv[:NP].astype(BF16)], axis=0)
            kv = _matmul(ckv_all, w_mla_ukv[j], tm=1024, tn=1024, out_dtype=BF16, name="kv_up")
            o_p = _attention(qa, qb, kv, kr[:NP].astype(BF16), kb[:NP].astype(BF16), vb[:NP].astype(BF16),
                             n_seq=N_PROMPT_SEQ, s_len=PROMPT_LEN, t_len=PROMPT_LEN, row0=0,
                             kv_row0=N_SAMPLE_SEQ * t_sample, name="attn_prompt")
            kr_cache = jnp.pad(cache_mla_krope[:, j], ((0, 0), (0, 0), (0, 128 - ROPE_A)))
            o_s = _attention(qa, qb, kv, _with_cache(kr_cache, kr[NP:]),
                             _with_cache(cache_gqa_k[:, j], kb[NP:]), _with_cache(cache_gqa_v[:, j], vb[NP:]),
                             n_seq=N_SAMPLE_SEQ, s_len=SAMPLE_LEN, t_len=t_sample, row0=NP, kv_row0=0,
                             name="attn_sample")
            x, h = _proj_res((o_p, o_s), w_out_ab_bf, j, x, mod, i, 2, g_post_mix[i], ffn_pre, tm=512,
                             name="out_ab")
            caches.append((ckv[:NP].reshape(N_PROMPT_SEQ, PROMPT_LEN, KV_LORA),
                           kr[:NP, :ROPE_A].reshape(N_PROMPT_SEQ, PROMPT_LEN, ROPE_A),
                           kb[:NP].reshape(N_PROMPT_SEQ, PROMPT_LEN, KV_B, HD_B),
                           vb[:NP].reshape(N_PROMPT_SEQ, PROMPT_LEN, KV_B, HD_B)))
        else:
            y = _matmul(h, w_in_c[j], tm=1024, tn=768, out_dtype=BF16, name="in_c")
            w_z = jnp.concatenate([w_gate_fwd_a[j], w_gate_bwd_a[j], jnp.zeros((D, 128 - 2 * GATE_RANK), F32)], axis=1)
            z = _matmul(h, w_z, tm=1024, tn=128, out_dtype=F32, name="gate_lowrank")
            gate_args = (w_gate_fwd_b[j], b_gate_fwd[j], w_gate_bwd_b[j], b_gate_bwd[j])
            of_p, ob_p, sf, sb = _gla(y, z, *gate_args, None, None,
                                      n_seq=N_PROMPT_SEQ, s_len=PROMPT_LEN, row0=0, name="gla_prompt")
            of_s, ob_s, _, _ = _gla(y, z, *gate_args, state_gla_fwd[:, j], state_gla_bwd[:, j],
                                    n_seq=N_SAMPLE_SEQ, s_len=SAMPLE_LEN, row0=NP, name="gla_sample")
            x, h = _gla_out((of_p, of_s), (ob_p, ob_s), y, g_gla_out[j], w_out_c_bf, j, x, mod, i,
                            g_post_mix[i], ffn_pre)
            states.append((sf, sb))
        act = _ffn_up(h, w_ffn_up, ffn_conv_w, ffn_conv_b, i)
        if i + 1 < depth:
            nxt = (g_pre_mix[i + 1], i + 1, 0, 1)
            x, h = _proj_res(act, w_down_bf, i, x, mod, i, 5, g_post_ffn[i], nxt, tm=256, name="ffn_down")
        else:
            y_prompt, _ = _proj_res(act, w_down_bf, i, x, mod, i, 5, g_post_ffn[i], None, tm=256,
                                    name="ffn_down_prompt", row0=0, n_rows=NP)
            y_sample, _ = _proj_res(act, w_down_bf, i, x, mod, i, 5, g_post_ffn[i], None, tm=256,
                                    name="ffn_down_sample", row0=NP, n_rows=NS)

    new_ckv, new_krope, new_k, new_v = (jnp.stack(t, axis=1) for t in zip(*caches))
    new_sf, new_sb = (jnp.stack(t, axis=1) for t in zip(*states))
    return (y_prompt.reshape(N_PROMPT_SEQ, PROMPT_LEN, D), y_sample.reshape(N_SAMPLE_SEQ, SAMPLE_LEN, D),
            new_ckv, new_krope, new_k, new_v, new_sf, new_sb)
```

```python
import functools
import math

import numpy as np
import jax
import jax.numpy as jnp
from jax import lax
from jax.experimental import pallas as pl
from jax.experimental.pallas import tpu as pltpu

D = 2048
N_PROMPT_SEQ, PROMPT_LEN = 16, 256
N_SAMPLE_SEQ, SAMPLE_LEN = 4, 2048
PAST = 512
GRID_W = 64
NP = N_PROMPT_SEQ * PROMPT_LEN
NS = N_SAMPLE_SEQ * SAMPLE_LEN
M_ROWS = NP + NS
EPS = 1e-6
ROPE_THETA = 10000.0

H_A, NOPE_A, ROPE_A, V_A = 8, 128, 64, 128
Q_LORA, KV_LORA = 512, 256
H_B, KV_B, HD_B = 8, 2, 128
IN_AB_PAD = 2560
H_C, DK_C, DV_C = 4, 256, 512
HK_C, HV_C = H_C * DK_C, H_C * DV_C
GATE_RANK = 16
GATE_TAU = 16.0
D_FF = 5632
GLA_CHUNK = 128
GLA_FINE = 3
N_MOD_ROWS = 8

VMEM_LIMIT_BYTES = 56 * 1024 * 1024
BF16 = jnp.bfloat16
F32 = jnp.float32


def _params(*sem):
    return pltpu.CompilerParams(dimension_semantics=sem, vmem_limit_bytes=VMEM_LIMIT_BYTES)


def _mod_row(i, tm):
    n_prompt_tiles = NP // tm
    per_seq = SAMPLE_LEN // tm
    return jnp.where(i < n_prompt_tiles, 0, 1 + (i - n_prompt_tiles) // per_seq)


def _mod_spec(layer, piece, tm, tile0=0):
    def index_map(i, *_):
        return ((layer * N_MOD_ROWS + _mod_row(tile0 + i, tm)) * 6 + piece, 0, 0)
    return pl.BlockSpec((None, 1, D), index_map)


def _rms(x, g):
    return x * lax.rsqrt(jnp.mean(x * x, axis=-1, keepdims=True) + EPS) * g


def _dot(a, b):
    return jnp.dot(a, b, preferred_element_type=F32)


def _dot_nt(a, b):
    return lax.dot_general(a, b, (((1,), (1,)), ((), ())), preferred_element_type=F32)


def _split2(x):
    hi = x.astype(BF16)
    lo = (x - hi.astype(F32)).astype(BF16)
    return hi, lo


def _mod_kernel(c_ref, w_ref, b_ref, o_ref):
    c = c_ref[...]
    a = c * (1.0 / (1.0 + jnp.exp(-c)))
    a_hi, a_lo = _split2(a)
    w_hi, w_lo = _split2(w_ref[...])
    r = _dot(jnp.concatenate([a_hi, a_lo], axis=0), w_hi)
    o_ref[...] = r[:N_MOD_ROWS] + r[N_MOD_ROWS:] + _dot(a_hi, w_lo) + b_ref[...]


def _modulation(cond, w_mod, b_mod):
    depth = w_mod.shape[0]
    n = w_mod.shape[2]
    tn = 1024
    return pl.pallas_call(
        _mod_kernel,
        grid=(depth, n // tn),
        in_specs=[pl.BlockSpec((N_MOD_ROWS, D), lambda l, j: (0, 0)),
                  pl.BlockSpec((None, D, tn), lambda l, j: (l, 0, j)),
                  pl.BlockSpec((None, 1, tn), lambda l, j: (l, 0, j))],
        out_specs=pl.BlockSpec((None, N_MOD_ROWS, tn), lambda l, j: (l, 0, j)),
        out_shape=jax.ShapeDtypeStruct((depth, N_MOD_ROWS, n), F32),
        compiler_params=_params("arbitrary", "arbitrary"),
        name="modulation",
    )(cond, w_mod, b_mod.reshape(depth, 1, n))


def _matmul_kernel(x_ref, w_ref, o_ref):
    o_ref[...] = _dot(x_ref[...].astype(BF16), w_ref[...].astype(BF16)).astype(o_ref.dtype)


def _matmul(x, w, *, tm, tn, out_dtype, name):
    m, k = x.shape
    n = w.shape[1]
    return pl.pallas_call(
        _matmul_kernel,
        grid=(n // tn, m // tm),
        in_specs=[pl.BlockSpec((tm, k), lambda j, i: (i, 0)),
                  pl.BlockSpec((k, tn), lambda j, i: (0, j))],
        out_specs=pl.BlockSpec((tm, tn), lambda j, i: (i, j)),
        out_shape=jax.ShapeDtypeStruct((m, n), out_dtype),
        compiler_params=_params("arbitrary", "arbitrary"),
        name=name,
    )(x, w)


def _rope_tables(rot_dim, tm):
    q = rot_dim // 4
    inv = ROPE_THETA ** (-np.arange(q, dtype=np.float64) / q)
    tok = np.arange(SAMPLE_LEN)
    ang_r = (tok // GRID_W)[:, None] * inv
    ang_c = (tok % GRID_W)[:, None] * inv
    ang = np.concatenate([ang_r, ang_r, ang_c, ang_c], axis=-1)
    cos, sin = np.cos(ang), np.sin(ang)
    first = (np.arange(rot_dim) % (2 * q)) < q
    sin_m = np.where(first, -sin, 0.0)
    sin_p = np.where(first, 0.0, sin)

    def widen(t, ident):
        full = np.full((tm + SAMPLE_LEN, 128), ident, np.float64)
        full[tm:, :rot_dim] = t
        return jnp.asarray(full, F32)

    return widen(cos, 1.0), widen(sin_m, 0.0), widen(sin_p, 0.0), q


def _rope128(x, cos, sin_m, sin_p, q):
    return (x * cos + pltpu.roll(x, 128 - q, 1) * sin_m + pltpu.roll(x, q, 1) * sin_p)


def _ab_post_kernel(y_ref, gq_ref, gkv_ref, gqn_ref, gkn_ref, wuq_ref,
                    ca_ref, ma_ref, pa_ref, cb_ref, mb_ref, pb_ref,
                    qa_ref, qb_ref, ckv_ref, kr_ref, kb_ref, vb_ref, *, qa_rot, qb_rot):
    ca, ma, pa = ca_ref[...], ma_ref[...], pa_ref[...]
    cb, mb, pb = cb_ref[...], mb_ref[...], pb_ref[...]
    scale_a = (NOPE_A + ROPE_A) ** -0.5 * math.log2(math.e)
    scale_b = HD_B ** -0.5 * math.log2(math.e)

    cq = _rms(y_ref[:, 0:Q_LORA], gq_ref[...])
    qa = _dot(cq.astype(BF16), wuq_ref[...].astype(BF16)) * scale_a
    for h in range(H_A):
        base = h * 256
        qa_ref[:, base:base + 128] = qa[:, base:base + 128].astype(qa_ref.dtype)
        qa_ref[:, base + 128:base + 256] = _rope128(
            qa[:, base + 128:base + 256], ca, ma, pa, qa_rot).astype(qa_ref.dtype)

    ckv_ref[...] = _rms(y_ref[:, 512:768], gkv_ref[...])
    kr_ref[...] = _rope128(y_ref[:, 2304:2432], ca, ma, pa, qa_rot)

    for h in range(H_B):
        xb = _rms(y_ref[:, 768 + h * 128:768 + (h + 1) * 128], gqn_ref[...])
        qb_ref[:, h * 128:(h + 1) * 128] = (
            _rope128(xb, cb, mb, pb, qb_rot) * scale_b).astype(qb_ref.dtype)
    for g in range(KV_B):
        xk = _rms(y_ref[:, 1792 + g * 128:1792 + (g + 1) * 128], gkn_ref[...])
        kb_ref[:, g * 128:(g + 1) * 128] = _rope128(xk, cb, mb, pb, qb_rot)
    vb_ref[...] = y_ref[:, 2048:2304]


def _ab_in_kernel(*refs, n_x, qa_rot, qb_rot):
    x_refs = refs[:n_x]
    g_ref, sh_ref, sc_ref, win_ref = refs[n_x:n_x + 4]
    y_ref = refs[-1]
    h = (_rms(_row_load(list(x_refs)), g_ref[...]) * (1.0 + sc_ref[...]) + sh_ref[...]).astype(BF16)
    y_ref[...] = _dot(h, win_ref[...])
    _ab_post_kernel(y_ref, *refs[n_x + 4:-1], qa_rot=qa_rot, qb_rot=qb_rot)


def _ab_in(x, g_pre, mod, layer, w_in_bf, g_q, g_kv, g_qn, g_kn, w_uq_bf):
    tm = 256
    ca, ma, pa, qa_rot = _rope_tables(ROPE_A, tm)
    cb, mb, pb, qb_rot = _rope_tables(HD_B, tm)
    n_prompt_tiles = NP // tm
    per_seq = SAMPLE_LEN // tm

    def tab_map(i):
        return (jnp.where(i < n_prompt_tiles, 0, 1 + (i - n_prompt_tiles) % per_seq), 0)

    tab = pl.BlockSpec((tm, 128), tab_map)
    row = lambda w: pl.BlockSpec((tm, w), lambda i: (i, 0))
    vec = lambda w: pl.BlockSpec((1, w), lambda i: (0, 0))
    resident = lambda arr: pl.BlockSpec(arr.shape, lambda i: (0, 0), pipeline_mode=pl.Buffered(1))
    x_specs, x_args = _row_specs(x, tm)
    return pl.pallas_call(
        functools.partial(_ab_in_kernel, n_x=len(x_args), qa_rot=qa_rot, qb_rot=qb_rot),
        grid=(M_ROWS // tm,),
        in_specs=x_specs + [vec(D), _mod_spec(layer, 0, tm), _mod_spec(layer, 1, tm), resident(w_in_bf),
                            vec(Q_LORA), vec(KV_LORA), vec(HD_B), vec(HD_B), resident(w_uq_bf),
                            tab, tab, tab, tab, tab, tab],
        out_specs=[row(H_A * 256), row(H_B * HD_B), row(KV_LORA), row(128),
                   row(KV_B * HD_B), row(KV_B * HD_B)],
        out_shape=[jax.ShapeDtypeStruct((M_ROWS, H_A * 256), BF16),
                   jax.ShapeDtypeStruct((M_ROWS, H_B * HD_B), BF16),
                   jax.ShapeDtypeStruct((M_ROWS, KV_LORA), F32),
                   jax.ShapeDtypeStruct((M_ROWS, 128), F32),
                   jax.ShapeDtypeStruct((M_ROWS, KV_B * HD_B), F32),
                   jax.ShapeDtypeStruct((M_ROWS, KV_B * HD_B), F32)],
        scratch_shapes=[pltpu.VMEM((tm, IN_AB_PAD), F32)],
        compiler_params=_params("arbitrary"),
        name="ab_in",
    )(*x_args, g_pre.reshape(1, D), mod, mod, w_in_bf, g_q.reshape(1, -1), g_kv.reshape(1, -1),
      g_qn.reshape(1, -1), g_kn.reshape(1, -1), w_uq_bf, ca, ma, pa, cb, mb, pb)


def _softmax_pv(s, v):
    m = jnp.max(s, axis=-1, keepdims=True)
    e = jnp.exp2(s - m)
    l = jnp.sum(e, axis=-1, keepdims=True)
    return _dot(e.astype(BF16), v) / l


def _attn_kernel(qa_ref, qb_ref, kv_ref, kr_ref, kb_ref, vb_ref, *rest):
    o_ref = rest[-1]
    kr = kr_ref[...]
    rep = H_B // KV_B
    n_heads = H_A + H_B

    def scores(h):
        if h < H_A:
            k = jnp.concatenate([kv_ref[:, h * 256:h * 256 + 128], kr], axis=1)
            return _dot_nt(qa_ref[:, h * 256:(h + 1) * 256], k)
        g = (h - H_A) // rep
        return _dot_nt(qb_ref[:, (h - H_A) * 128:(h - H_A + 1) * 128], kb_ref[:, g * 128:(g + 1) * 128])

    def values(h):
        if h < H_A:
            return kv_ref[:, h * 256 + 128:(h + 1) * 256]
        g = (h - H_A) // rep
        return vb_ref[:, g * 128:(g + 1) * 128]

    s_next = scores(0)
    for h in range(n_heads):
        s_cur = s_next
        if h + 1 < n_heads:
            s_next = scores(h + 1)
        o_ref[:, h * 128:(h + 1) * 128] = _softmax_pv(s_cur, values(h)).astype(o_ref.dtype)


def _attention(qa, qb, kv, kr, kb, vb, *, n_seq, s_len, t_len, row0, kv_row0, name):
    tq = 256
    nq = s_len // tq
    q0 = row0 // tq
    kv0 = kv_row0 // t_len
    qspec = lambda w: pl.BlockSpec((tq, w), lambda b, i: (q0 + b * nq + i, 0))
    kspec = lambda w, b0=0: pl.BlockSpec((t_len, w), lambda b, i: (b0 + b, 0), pipeline_mode=pl.Buffered(1))
    return pl.pallas_call(
        _attn_kernel,
        grid=(n_seq, nq),
        in_specs=[qspec(H_A * 256), qspec(H_B * HD_B), kspec(H_A * 256, kv0), kspec(128),
                  kspec(KV_B * HD_B), kspec(KV_B * HD_B)],
        out_specs=pl.BlockSpec((tq, D), lambda b, i: (b * nq + i, 0)),
        out_shape=jax.ShapeDtypeStruct((n_seq * s_len, D), BF16),
        compiler_params=_params("arbitrary", "arbitrary"),
        name=name,
    )(qa, qb, kv, kr, kb, vb)


def _row_specs(arr, tm, tile0=0):
    if not isinstance(arr, tuple):
        return [pl.BlockSpec((tm, arr.shape[1]), lambda i, *_: (tile0 + i, 0))], [arr]
    npt = NP // tm
    w = arr[0].shape[1]
    return ([pl.BlockSpec((tm, w), lambda i, *_: (jnp.minimum(tile0 + i, npt - 1), 0)),
             pl.BlockSpec((tm, w), lambda i, *_: (jnp.maximum(tile0 + i - npt, 0), 0))], list(arr))


def _row_load(refs, tile0=0):
    if len(refs) == 1:
        return refs[0][...]
    tm = refs[0].shape[0]
    return jnp.where(tile0 + pl.program_id(0) < NP // tm, refs[0][...], refs[1][...])


def _residual_epilogue(acc, x, gpost_ref, gate_ref, nxt, xo_ref, ho_ref, rows=slice(None)):
    x_new = x + gate_ref[...] * _rms(acc, gpost_ref[...])
    xo_ref[rows, :] = x_new
    if nxt is not None:
        gn_ref, sh_ref, sc_ref = nxt
        ho_ref[rows, :] = (_rms(x_new, gn_ref[...]) * (1.0 + sc_ref[...]) + sh_ref[...]).astype(ho_ref.dtype)


def _proj_res_kernel(*refs, n_a, n_x, emit_h, tile0, row_split):
    a_refs, w_ref, x_refs, rest = refs[:n_a], refs[n_a], refs[n_a + 1:n_a + 1 + n_x], refs[n_a + 1 + n_x:]
    if emit_h:
        gpost_ref, gate_ref, gn_ref, sh_ref, sc_ref, xo_ref, ho_ref = rest
        nxt = (gn_ref, sh_ref, sc_ref)
    else:
        gpost_ref, gate_ref, xo_ref = rest
        nxt, ho_ref = None, None
    a = _row_load(a_refs, tile0)
    x = _row_load(x_refs, tile0)
    rows = [slice(r * a.shape[0] // row_split, (r + 1) * a.shape[0] // row_split) for r in range(row_split)]
    accs = [_dot(a[rs], w_ref[...]) for rs in rows]
    for rs, acc in zip(rows, accs):
        _residual_epilogue(acc, x[rs], gpost_ref, gate_ref, nxt, xo_ref, ho_ref, rs)


def _res_specs(tm, tile0, n_rows, mod, layer, gate_piece, g_post, nxt):
    vec = pl.BlockSpec((1, D), lambda i, *_: (0, 0))
    specs = [vec, _mod_spec(layer, gate_piece, tm, tile0)]
    args = [g_post.reshape(1, D), mod]
    if nxt is not None:
        g_next, layer_next, sh_piece, sc_piece = nxt
        specs += [vec, _mod_spec(layer_next, sh_piece, tm, tile0), _mod_spec(layer_next, sc_piece, tm, tile0)]
        args += [g_next.reshape(1, D), mod, mod]
    out_specs = [pl.BlockSpec((tm, D), lambda i, *_: (i, 0))]
    out_shape = [jax.ShapeDtypeStruct((n_rows, D), F32)]
    if nxt is not None:
        out_specs.append(pl.BlockSpec((tm, D), lambda i, *_: (i, 0)))
        out_shape.append(jax.ShapeDtypeStruct((n_rows, D), BF16))
    return specs, args, out_specs, out_shape


def _resident_weight_spec(kdim, w_layer):
    return pl.BlockSpec((None, kdim, D), lambda i: (w_layer, 0, 0), pipeline_mode=pl.Buffered(1))


def _proj_res(a, w, w_layer, x, mod, layer, gate_piece, g_post, nxt, *, tm, name, row0=0, n_rows=M_ROWS,
              row_split=1):
    tile0 = row0 // tm
    a_specs, a_args = _row_specs(a, tm, tile0)
    x_specs, x_args = _row_specs(x, tm, tile0)
    kdim = a_args[0].shape[1]
    specs, args, out_specs, out_shape = _res_specs(tm, tile0, n_rows, mod, layer, gate_piece, g_post, nxt)
    out = pl.pallas_call(
        functools.partial(_proj_res_kernel, n_a=len(a_args), n_x=len(x_args), emit_h=nxt is not None, tile0=tile0,
                          row_split=row_split),
        grid=(n_rows // tm,),
        in_specs=a_specs + [_resident_weight_spec(kdim, w_layer)] + x_specs + specs,
        out_specs=out_specs,
        out_shape=out_shape,
        compiler_params=_params("arbitrary"),
        name=name,
    )(*a_args, w, *x_args, *args)
    return out if nxt is not None else (out[0], None)


def _ffn_up_kernel(h_ref, hp_ref, hn_ref, wa_ref, wg_ref, cwa_ref, cwg_ref, cba_ref, cbg_ref,
                   mp_ref, mn_ref, o_ref, wbf_ref):
    tm = h_ref.shape[0]
    halo = hp_ref.shape[0]

    @pl.when(pl.program_id(1) == 0)
    def _():
        wbf_ref[0] = wa_ref[...].astype(BF16)
        wbf_ref[1] = wg_ref[...].astype(BF16)

    h = h_ref[...]
    win = jnp.concatenate([hp_ref[...], h, hn_ref[...]], axis=0)
    row8 =lax.broadcasted_iota(jnp.int32, (8, 1), 0)
    n_seg = tm // PROMPT_LEN

    def patch_edges(shifted, tile_edge_row, at_start, mask_ref):
        pieces = []
        for seg in range(n_seg):
            r = seg * PROMPT_LEN if at_start else (seg + 1) * PROMPT_LEN - 1
            g0 = r - r % 8
            grp = shifted[g0:g0 + 8]
            at_tile_edge = seg == 0 if at_start else seg == n_seg - 1
            src = tile_edge_row if at_tile_edge else grp
            grp = jnp.where(row8 == r % 8, src * mask_ref[r:r + 1, :], grp)
            if at_start:
                pieces += [grp, shifted[g0 + 8:(seg + 1) * PROMPT_LEN]]
            else:
                pieces += [shifted[seg * PROMPT_LEN:g0], grp]
        return jnp.concatenate(pieces, axis=0)

    def branch(which, cw_ref, cb_ref):
        w = wbf_ref[which]
        u_all = _dot(win, w)
        u = u_all[halo:halo + tm]
        prev = patch_edges(pltpu.roll(u, 1, 0), u_all[halo - 1:halo], True, mp_ref)
        nxt = patch_edges(pltpu.roll(u, tm - 1, 0), u_all[halo + tm:halo + tm + 1], False, mn_ref)
        cw = cw_ref[...]
        return prev * cw[0:1] + u * cw[1:2] + nxt * cw[2:3] + cb_ref[...]

    a = branch(0, cwa_ref, cba_ref)
    g = branch(1, cwg_ref, cbg_ref)
    o_ref[...] = (g * (1.0 / (1.0 + jnp.exp(-g))) * a).astype(o_ref.dtype)


def _seq_edge_masks():
    pos = np.concatenate([np.tile(np.arange(PROMPT_LEN), N_PROMPT_SEQ),
                          np.tile(np.arange(SAMPLE_LEN), N_SAMPLE_SEQ)])
    lens = np.concatenate([np.full(NP, PROMPT_LEN), np.full(NS, SAMPLE_LEN)])
    has_prev = (pos > 0).astype(np.float32).reshape(M_ROWS, 1)
    has_next = (pos < lens - 1).astype(np.float32).reshape(M_ROWS, 1)
    return jnp.asarray(has_prev), jnp.asarray(has_next)


def _ffn_up(h, w_up, conv_w, conv_b, layer):
    tm, tn, halo = 1024, 512, 16
    nj = D_FF // tn
    r = tm // halo
    last = M_ROWS // halo - 1
    has_prev, has_next = _seq_edge_masks()
    conv_b = conv_b.reshape(conv_b.shape[0], 1, 2 * D_FF)
    col = lambda rows, half: pl.BlockSpec((None, rows, tn), lambda j, i: (layer, 0, j + half * nj))
    return pl.pallas_call(
        _ffn_up_kernel,
        grid=(nj, M_ROWS // tm),
        in_specs=[pl.BlockSpec((tm, D), lambda j, i: (i, 0)),
                  pl.BlockSpec((halo, D), lambda j, i: (jnp.maximum(i * r - 1, 0), 0)),
                  pl.BlockSpec((halo, D), lambda j, i: (jnp.minimum((i + 1) * r, last), 0)),
                  col(D, 0), col(D, 1), col(3, 0), col(3, 1), col(1, 0), col(1, 1),
                  pl.BlockSpec((tm, 1), lambda j, i: (i, 0)),
                  pl.BlockSpec((tm, 1), lambda j, i: (i, 0))],
        out_specs=pl.BlockSpec((tm, tn), lambda j, i: (i, j)),
        out_shape=jax.ShapeDtypeStruct((M_ROWS, D_FF), BF16),
        scratch_shapes=[pltpu.VMEM((2, D, tn), BF16)],
        compiler_params=_params("arbitrary", "arbitrary"),
        name="ffn_up",
    )(h, h, h, w_up, w_up, conv_w, conv_w, conv_b, conv_b, has_prev, has_next)


def _gla_masks(c):
    levels = int(math.log2(c))
    t = np.arange(c)
    pair = [t[:, None] == t[None, :]]
    for l in range(1, levels + 1):
        blk = t >> l
        up = ((t >> (l - 1)) & 1) == 1
        pair.append((blk[:, None] == blk[None, :]) & up[:, None] & ~up[None, :])
    pair = np.stack(pair).astype(np.float32)
    pair = np.stack([pair, pair.transpose(0, 2, 1)])
    mats = []
    for l in range(1, GLA_FINE + 1):
        split = ((t >> l) << l) + (1 << (l - 1)) - 1
        lo, hi = np.minimum(t, split), np.maximum(t, split)
        mats.append((t[None, :] > lo[:, None]) & (t[None, :] <= hi[:, None]))
    same = (t >> GLA_FINE)[:, None] == (t >> GLA_FINE)[None, :]
    mats += [same & (t[None, :] <= t[:, None]), same]
    fine = np.stack([np.concatenate(mats, axis=0),
                     np.concatenate([m[::-1, ::-1] for m in mats], axis=0)]).astype(np.float32)
    return jnp.asarray(pair), jnp.asarray(fine, BF16)


def _gla_chain(q, k, v, za, wb, ba, pair_ref, fine_ref, s_ref, reverse):
    c = q.shape[0]
    levels = pair_ref.shape[1] - 1
    za_hi, za_lo = _split2(za)
    wb_hi, wb_lo = _split2(wb)
    z = _dot(za_hi, wb_hi) + _dot(za_lo, wb_hi) + _dot(za_hi, wb_lo) + ba
    x = (jnp.minimum(z, 0.0) - jnp.log(1.0 + jnp.exp(-jnp.abs(z)))) * (math.log2(math.e) / GATE_TAU)
    yield None

    rev = 1 if reverse else 0
    first_i, second_i = (1, 0) if reverse else (0, 1)
    x_hi, x_lo = _split2(x)
    fine = _dot(fine_ref[rev], jnp.concatenate([x_hi, x_lo], axis=1))
    yield None
    fine = fine[:, :DK_C] + fine[:, DK_C:]
    p, t = fine[GLA_FINE * c:(GLA_FINE + 1) * c], fine[(GLA_FINE + 1) * c:(GLA_FINE + 2) * c]
    k_bf = k.astype(BF16)
    a = pair_ref[rev, 0] * _dot_nt(q.astype(BF16), k_bf)
    for l in range(1, levels + 1):
        s = 1 << (l - 1)
        if l <= GLA_FINE:
            e = jnp.exp2(fine[(l - 1) * c:l * c])
            a = a + pair_ref[rev, l] * _dot_nt((q * e).astype(BF16), (k * e).astype(BF16))
        else:
            nb = c // (2 * s)
            split = lambda arr: arr.reshape(nb, 2, s, arr.shape[-1])
            join = lambda f, sec: jnp.stack([f, sec] if first_i == 0 else [sec, f], axis=1)
            p4, t4 = split(p), split(t)
            p_f, p_s, t_f, t_s = p4[:, first_i], p4[:, second_i], t4[:, first_i], t4[:, second_i]
            q_s = split(q)[:, second_i] * jnp.exp2(p_s)
            k_f = split(k)[:, first_i] * jnp.exp2(t_f - p_f)
            k_all = join(k_f.astype(BF16), split(k_bf)[:, second_i]).reshape(c, DK_C)
            r = _dot_nt(q_s.reshape(c // 2, DK_C).astype(BF16), k_all).reshape(nb, s, c)
            a4 = a.reshape(nb, 2, s, c)
            mask = pair_ref[rev, l].reshape(nb, 2, s, c)[:, second_i]
            a = join(a4[:, first_i], a4[:, second_i] + mask * r).reshape(c, c)
            t_new = t_f + t_s
            p = join(p_f, p_s + t_f).reshape(c, DK_C)
            t = join(t_new, t_new).reshape(c, DK_C)
        yield None
    s_old = s_ref[...]
    o_inter = _dot((q * jnp.exp2(p)).astype(BF16), s_old.astype(BF16))
    kd_t = (k * jnp.exp2(t - p)).T.astype(BF16)
    both = _dot(jnp.concatenate([a.astype(BF16), kd_t], axis=0), v)
    dec_col = jnp.exp2(t[0:8]).T[:, 0:1]
    s_ref[...] = s_old * dec_col + both[c:]
    yield both[:c] + o_inter


def _gla_kernel(*refs, zero_init):
    (qf_ref, kf_ref, vf_ref, zf_ref, qb_ref, kb_ref, vb_ref, zb_ref,
     wbf_ref, baf_ref, wbb_ref, bab_ref, pair_ref, fine_ref) = refs[:14]
    rest = refs[14:]
    if not zero_init:
        sfi_ref, sbi_ref = rest[:2]
        rest = rest[2:]
    of_ref, ob_ref, sfo_ref, sbo_ref, sf_ref, sb_ref = rest
    ci = pl.program_id(1)

    @pl.when(ci == 0)
    def _():
        if zero_init:
            sf_ref[...] = jnp.zeros_like(sf_ref)
            sb_ref[...] = jnp.zeros_like(sb_ref)
        else:
            sf_ref[...] = sfi_ref[...]
            sb_ref[...] = sbi_ref[...]

    scale = DK_C ** -0.5
    chains = []
    for h in range(H_C):
        ks, vs = slice(h * DK_C, (h + 1) * DK_C), slice(h * DV_C, (h + 1) * DV_C)
        chains.append((of_ref, vs, _gla_chain(
            qf_ref[:, ks].astype(F32) * scale, kf_ref[:, ks].astype(F32), vf_ref[:, vs],
            zf_ref[:, 0:GATE_RANK], wbf_ref[:, ks], baf_ref[:, ks], pair_ref, fine_ref, sf_ref.at[h], False)))
        chains.append((ob_ref, vs, _gla_chain(
            qb_ref[:, ks].astype(F32) * scale, kb_ref[:, ks].astype(F32), vb_ref[:, vs],
            zb_ref[:, GATE_RANK:2 * GATE_RANK], wbb_ref[:, ks], bab_ref[:, ks], pair_ref, fine_ref,
            sb_ref.at[h], True)))
    live = True
    while live:
        for o_ref, vs, chain in chains:
            out = next(chain, "done")
            if isinstance(out, str):
                live = False
            elif out is not None:
                o_ref[:, vs] = out.astype(o_ref.dtype)

    @pl.when(ci == pl.num_programs(1) - 1)
    def _():
        sfo_ref[...] = sf_ref[...]
        sbo_ref[...] = sb_ref[...]


def _gla(y, z, wb_f, ba_f, wb_b, ba_b, s_f, s_b, *, n_seq, s_len, row0, name):
    c = GLA_CHUNK
    n = s_len // c
    r0 = row0 // c
    zero_init = s_f is None
    pair, fine = _gla_masks(c)
    fwd = lambda b, i: b * n + i
    bwd = lambda b, i: b * n + (n - 1 - i)

    def dir_specs(rows):
        return [pl.BlockSpec((c, HK_C), lambda b, i: (r0 + rows(b, i), 0)),
                pl.BlockSpec((c, HK_C), lambda b, i: (r0 + rows(b, i), 1)),
                pl.BlockSpec((c, HV_C), lambda b, i: (r0 + rows(b, i), 1)),
                pl.BlockSpec((c, 128), lambda b, i: (r0 + rows(b, i), 0))]

    full = lambda arr: pl.BlockSpec(arr.shape, lambda b, i: (0,) * arr.ndim)
    state = pl.BlockSpec((None, H_C, DK_C, DV_C), lambda b, i: (b, 0, 0, 0))
    ba_f, ba_b = ba_f.reshape(1, -1), ba_b.reshape(1, -1)
    in_specs = (dir_specs(fwd) + dir_specs(bwd)
                + [full(wb_f), full(ba_f), full(wb_b), full(ba_b), full(pair), full(fine)])
    args = [y, y, y, z, y, y, y, z, wb_f, ba_f, wb_b, ba_b, pair, fine]
    if not zero_init:
        in_specs += [state, state]
        args += [s_f, s_b]
    return pl.pallas_call(
        functools.partial(_gla_kernel, zero_init=zero_init),
        grid=(n_seq, n),
        in_specs=in_specs,
        out_specs=[pl.BlockSpec((c, HV_C), lambda b, i: (fwd(b, i), 0)),
                   pl.BlockSpec((c, HV_C), lambda b, i: (bwd(b, i), 0)),
                   state, state],
        out_shape=[jax.ShapeDtypeStruct((n_seq * s_len, HV_C), BF16),
                   jax.ShapeDtypeStruct((n_seq * s_len, HV_C), BF16),
                   jax.ShapeDtypeStruct((n_seq, H_C, DK_C, DV_C), F32),
                   jax.ShapeDtypeStruct((n_seq, H_C, DK_C, DV_C), F32)],
        scratch_shapes=[pltpu.VMEM((H_C, DK_C, DV_C), F32), pltpu.VMEM((H_C, DK_C, DV_C), F32)],
        compiler_params=_params("arbitrary", "arbitrary"),
        name=name,
    )(*args)


def _gla_out_kernel(ofp_ref, ofs_ref, obp_ref, obs_ref, r_ref, go_ref, w_ref, x_ref, gpost_ref, gate_ref,
                    gn_ref, sh_ref, sc_ref, xo_ref, ho_ref, a_ref):
    o_all = _row_load([ofp_ref, ofs_ref]).astype(F32) + _row_load([obp_ref, obs_ref]).astype(F32)
    for h in range(H_C):
        vs = slice(h * DV_C, (h + 1) * DV_C)
        r = r_ref[:, vs].astype(F32)
        a = _rms(o_all[:, vs], go_ref[...]) * (r * (1.0 / (1.0 + jnp.exp(-r))))
        a_ref[:, vs] = a.astype(BF16)
    _residual_epilogue(_dot(a_ref[...], w_ref[...]), x_ref[...], gpost_ref, gate_ref,
                       (gn_ref, sh_ref, sc_ref), xo_ref, ho_ref)


def _gla_out(o_f, o_b, y, g_out, w, w_layer, x, mod, layer, g_post, nxt):
    tm = 256
    r_blk = (2 * HK_C + HV_C) // HV_C
    specs, args, out_specs, out_shape = _res_specs(tm, 0, M_ROWS, mod, layer, 2, g_post, nxt)
    of_specs, of_args = _row_specs(o_f, tm)
    ob_specs, ob_args = _row_specs(o_b, tm)
    return pl.pallas_call(
        _gla_out_kernel,
        grid=(M_ROWS // tm,),
        in_specs=of_specs + ob_specs + [
            pl.BlockSpec((tm, HV_C), lambda i: (i, r_blk)),
            pl.BlockSpec((1, DV_C), lambda i: (0, 0)),
            _resident_weight_spec(HV_C, w_layer),
            pl.BlockSpec((tm, D), lambda i: (i, 0))] + specs,
        out_specs=out_specs,
        out_shape=out_shape,
        scratch_shapes=[pltpu.VMEM((tm, HV_C), BF16)],
        compiler_params=_params("arbitrary"),
        name="gla_out",
    )(*of_args, *ob_args, y, g_out.reshape(1, -1), w, x, *args)


def _with_cache(cache, new):
    w = new.shape[-1]
    cat = jnp.concatenate([cache.reshape(N_SAMPLE_SEQ, PAST, w).astype(BF16),
                           new.reshape(N_SAMPLE_SEQ, SAMPLE_LEN, w).astype(BF16)], axis=1)
    return cat.reshape(N_SAMPLE_SEQ * (PAST + SAMPLE_LEN), w)


def kernel(x_prompt, x_sample, cache_mla_ckv, cache_mla_krope, cache_gqa_k, cache_gqa_v, state_gla_fwd, state_gla_bwd, c, c_ctx, w_mod, b_mod, g_pre_mix, g_post_mix, g_pre_ffn, g_post_ffn, w_in_ab, g_mla_q, g_mla_kv, w_mla_uq, w_mla_ukv, g_gqa_q, g_gqa_k, w_out_ab, w_in_c, w_gate_fwd_a, w_gate_fwd_b, b_gate_fwd, w_gate_bwd_a, w_gate_bwd_b, b_gate_bwd, g_gla_out, w_out_c, w_ffn_up, ffn_conv_w, ffn_conv_b, w_ffn_down):
    depth = w_mod.shape[0]
    cond = jnp.concatenate([c_ctx[None], c, jnp.zeros((N_MOD_ROWS - 1 - N_SAMPLE_SEQ, D), F32)], axis=0)
    mod = _modulation(cond, w_mod, b_mod).reshape(depth * N_MOD_ROWS * 6, 1, D)
    w_out_ab_bf, w_out_c_bf, w_down_bf = w_out_ab.astype(BF16), w_out_c.astype(BF16), w_ffn_down.astype(BF16)
    t_sample = PAST + SAMPLE_LEN

    x = (x_prompt.reshape(NP, D), x_sample.reshape(NS, D))
    h = None
    caches, states = [], []
    for i in range(depth):
        j = i // 2
        ffn_pre = (g_pre_ffn[i], i, 3, 4)
        if i % 2 == 0:
            w = w_in_ab[j]
            w_in = jnp.concatenate(
                [w[:, 0:768], w[:, 832:2368], w[:, 768:832], jnp.zeros((D, IN_AB_PAD - 2368), F32)],
                axis=1).astype(BF16)
            wq = w_mla_uq[j].reshape(Q_LORA, H_A, NOPE_A + ROPE_A)
            w_uq = jnp.concatenate([wq, jnp.zeros((Q_LORA, H_A, 256 - NOPE_A - ROPE_A), F32)],
                                   axis=2).reshape(Q_LORA, H_A * 256).astype(BF16)
            qa, qb, ckv, kr, kb, vb = _ab_in(x, g_pre_mix[i], mod, i, w_in, g_mla_q[j], g_mla_kv[j],
                                             g_gqa_q[j], g_gqa_k[j], w_uq)
            ckv_all = jnp.concatenate([_with_cache(cache_mla_ckv[:, j], ckv[NP:]), ckv[:NP].astype(BF16)], axis=0)
            kv = _matmul(ckv_all, w_mla_ukv[j], tm=1024, tn=1024, out_dtype=BF16, name="kv_up")
            o_p = _attention(qa, qb, kv, kr[:NP].astype(BF16), kb[:NP].astype(BF16), vb[:NP].astype(BF16),
                             n_seq=N_PROMPT_SEQ, s_len=PROMPT_LEN, t_len=PROMPT_LEN, row0=0,
                             kv_row0=N_SAMPLE_SEQ * t_sample, name="attn_prompt")
            kr_cache = jnp.pad(cache_mla_krope[:, j], ((0, 0), (0, 0), (0, 128 - ROPE_A)))
            o_s = _attention(qa, qb, kv, _with_cache(kr_cache, kr[NP:]),
                             _with_cache(cache_gqa_k[:, j], kb[NP:]), _with_cache(cache_gqa_v[:, j], vb[NP:]),
                             n_seq=N_SAMPLE_SEQ, s_len=SAMPLE_LEN, t_len=t_sample, row0=NP, kv_row0=0,
                             name="attn_sample")
            x, h = _proj_res((o_p, o_s), w_out_ab_bf, j, x, mod, i, 2, g_post_mix[i], ffn_pre, tm=512,
                             row_split=4, name="out_ab")
            caches.append((ckv[:NP].reshape(N_PROMPT_SEQ, PROMPT_LEN, KV_LORA),
                           kr[:NP, :ROPE_A].reshape(N_PROMPT_SEQ, PROMPT_LEN, ROPE_A),
                           kb[:NP].reshape(N_PROMPT_SEQ, PROMPT_LEN, KV_B, HD_B),
                           vb[:NP].reshape(N_PROMPT_SEQ, PROMPT_LEN, KV_B, HD_B)))
        else:
            y = _matmul(h, w_in_c[j], tm=1024, tn=768, out_dtype=BF16, name="in_c")
            w_z = jnp.concatenate([w_gate_fwd_a[j], w_gate_bwd_a[j], jnp.zeros((D, 128 - 2 * GATE_RANK), F32)], axis=1)
            z = _matmul(h, w_z, tm=1024, tn=128, out_dtype=F32, name="gate_lowrank")
            gate_args = (w_gate_fwd_b[j], b_gate_fwd[j], w_gate_bwd_b[j], b_gate_bwd[j])
            of_p, ob_p, sf, sb = _gla(y, z, *gate_args, None, None,
                                      n_seq=N_PROMPT_SEQ, s_len=PROMPT_LEN, row0=0, name="gla_prompt")
            of_s, ob_s, _, _ = _gla(y, z, *gate_args, state_gla_fwd[:, j], state_gla_bwd[:, j],
                                    n_seq=N_SAMPLE_SEQ, s_len=SAMPLE_LEN, row0=NP, name="gla_sample")
            x, h = _gla_out((of_p, of_s), (ob_p, ob_s), y, g_gla_out[j], w_out_c_bf, j, x, mod, i,
                            g_post_mix[i], ffn_pre)
            states.append((sf, sb))
        act = _ffn_up(h, w_ffn_up, ffn_conv_w, ffn_conv_b, i)
        if i + 1 < depth:
            nxt = (g_pre_mix[i + 1], i + 1, 0, 1)
            x, h = _proj_res(act, w_down_bf, i, x, mod, i, 5, g_post_ffn[i], nxt, tm=256, name="ffn_down")
        else:
            y_prompt, _ = _proj_res(act, w_down_bf, i, x, mod, i, 5, g_post_ffn[i], None, tm=256,
                                    name="ffn_down_prompt", row0=0, n_rows=NP)
            y_sample, _ = _proj_res(act, w_down_bf, i, x, mod, i, 5, g_post_ffn[i], None, tm=256,
                                    name="ffn_down_sample", row0=NP, n_rows=NS)

    new_ckv, new_krope, new_k, new_v = (jnp.stack(t, axis=1) for t in zip(*caches))
    new_sf, new_sb = (jnp.stack(t, axis=1) for t in zip(*states))
    return (y_prompt.reshape(N_PROMPT_SEQ, PROMPT_LEN, D), y_sample.reshape(N_SAMPLE_SEQ, SAMPLE_LEN, D),
            new_ckv, new_krope, new_k, new_v, new_sf, new_sb)
```
